```python
import jax, jax.numpy as jnp
from jax import lax
import numpy as np

D_MODEL = 1024
BATCH = 1
SEQ = 16384
DEPTH = 2
DEC_BATCH = 32
DEC_SEQ = 4
PAST_LEN = 16384
PAGE_SIZE = 128

N_MIXERS = 2
N_POOL_LAYERS = (DEPTH + 1) // 2
N_SB_LAYERS = DEPTH // 2
POOL_WINDOWS = (2, 4, 8, 16)
N_POOL_GROUPS = len(POOL_WINDOWS)
POOL_GROUP_DIM = D_MODEL // N_POOL_GROUPS
POOL_STATE_LEN = max(POOL_WINDOWS) - 1
HEAD_DIM = 64
N_HEADS = D_MODEL // HEAD_DIM
D_FF = ((8 * D_MODEL + 3 * 256 - 1) // (3 * 256)) * 256
Q_BLOCK = 128
EPS = 1e-6
SB_SCALE = HEAD_DIM ** -0.5
SB_BIAS_INIT = -8.0

kernel_name = "pool_stickbreak_hybrid_step"


def rms_norm(x, g):
    xf = x.astype(jnp.float32)
    y = xf * lax.rsqrt(jnp.mean(xf * xf, axis=-1, keepdims=True) + EPS)
    return (y * g.astype(jnp.float32)).astype(x.dtype)


def ada_modulation(c, w_ada, b_ada):
    m = jax.nn.silu(c) @ w_ada + b_ada
    return jnp.split(m[:, None, :], 6, axis=-1)


def modulate(x, g, shift, scale):
    return rms_norm(x, g) * (1 + scale) + shift


def swiglu(h, w_gu, w_down):
    g, u = jnp.split(h @ w_gu, 2, axis=-1)
    return (jax.nn.silu(g) * u) @ w_down


def pool_mix(u, prefix, pos0, w_pool, pool_scale):
    B, T, D = u.shape
    L = prefix.shape[1]
    ext = jnp.concatenate([prefix, u], axis=1)
    extf = ext.astype(jnp.float32)
    cs = jnp.concatenate([jnp.zeros((B, 1, D), jnp.float32), jnp.cumsum(extf, axis=1)], axis=1)
    end = cs[:, L + 1:L + 1 + T]
    pos = pos0 + jnp.arange(T)
    means = []
    for g, w in enumerate(POOL_WINDOWS):
        sl = slice(g * POOL_GROUP_DIM, (g + 1) * POOL_GROUP_DIM)
        start = cs[:, L + 1 - w:L + 1 - w + T, sl]
        cnt = jnp.minimum(pos + 1, w).astype(jnp.float32)[None, :, None]
        means.append((end[..., sl] - start) / cnt)
    pooled = (jnp.concatenate(means, axis=-1) - extf[:, L:]).astype(u.dtype)
    pooled = pooled.reshape(B, T, N_POOL_GROUPS, POOL_GROUP_DIM)
    out = jnp.einsum('btgi,gio->btgo', pooled, w_pool).reshape(B, T, D) * pool_scale
    return out, ext[:, -L:]


def sb_qkv(h, w_qkv, g_q, g_k):
    B, T, _ = h.shape
    qkv = (h @ w_qkv).reshape(B, T, 3, N_HEADS, HEAD_DIM)
    return rms_norm(qkv[:, :, 0], g_q), rms_norm(qkv[:, :, 1], g_k), qkv[:, :, 2]


def stick_breaking_weights(z, mask):
    log_beta = jax.nn.log_sigmoid(z)
    log_rest = jnp.where(mask, jax.nn.log_sigmoid(-z), 0.0)
    later = lax.cumsum(log_rest, axis=z.ndim - 1, reverse=True) - log_rest
    return jnp.where(mask, jnp.exp(log_beta + later), 0.0)


def sb_attend_prompt(q, k, v, bias):
    B, S, H, Dh = q.shape
    nb = S // Q_BLOCK
    qb = q.reshape(B, nb, Q_BLOCK, H, Dh).transpose(1, 0, 2, 3, 4)
    k_pos = jnp.arange(S)
    b = bias.astype(jnp.float32)[None, :, None, None]

    def block(args):
        q_blk, i = args
        z = jnp.einsum('bqhd,bkhd->bhqk', q_blk, k, preferred_element_type=jnp.float32) * SB_SCALE + b
        q_pos = i * Q_BLOCK + jnp.arange(Q_BLOCK)
        a = stick_breaking_weights(z, k_pos[None, :] < q_pos[:, None])
        return jnp.einsum('bhqk,bkhd->bqhd', a.astype(v.dtype), v)

    o = lax.map(block, (qb, jnp.arange(nb)))
    return o.transpose(1, 0, 2, 3, 4).reshape(B, S, H, Dh)


def sb_attend_sample(q, k_new, v_new, k_past, v_past, bias):
    P, T = k_past.shape[1], q.shape[1]
    b = bias.astype(jnp.float32)[None, :, None, None]
    z = jnp.concatenate([
        jnp.einsum('bqhd,bkhd->bhqk', q, k_past, preferred_element_type=jnp.float32),
        jnp.einsum('bqhd,bkhd->bhqk', q, k_new, preferred_element_type=jnp.float32)], axis=-1) * SB_SCALE + b
    q_pos = P + jnp.arange(T)
    k_pos = jnp.arange(P + T)
    a = stick_breaking_weights(z, k_pos[None, :] < q_pos[:, None]).astype(v_new.dtype)
    return (jnp.einsum('bhqk,bkhd->bqhd', a[..., :P], v_past)
            + jnp.einsum('bhqk,bkhd->bqhd', a[..., P:], v_new))


def setup_inputs(seed: int = 0) -> dict:
    key = jax.random.key(seed)
    ks = jax.random.split(key, 24)
    f32 = jnp.float32
    n_pages = PAST_LEN // PAGE_SIZE
    n_used = DEC_BATCH * n_pages
    n_phys = n_used + n_used // 4
    nrm = lambda k, shape, s=1.0: (jax.random.normal(k, shape, f32) * s)
    page_table = jax.random.permutation(ks[5], n_phys)[:n_used].reshape(DEC_BATCH, n_pages).astype(jnp.int32)
    return {
        "x_prompt": nrm(ks[0], (BATCH, SEQ, D_MODEL)),
        "x_sample": nrm(ks[1], (DEC_BATCH, DEC_SEQ, D_MODEL)),
        "state_pool": nrm(ks[2], (N_POOL_LAYERS, DEC_BATCH, POOL_STATE_LEN, D_MODEL)),
        "cache_k": nrm(ks[3], (N_SB_LAYERS, n_phys, PAGE_SIZE, N_HEADS, HEAD_DIM)),
        "cache_v": nrm(ks[4], (N_SB_LAYERS, n_phys, PAGE_SIZE, N_HEADS, HEAD_DIM)),
        "page_table": page_table,
        "c_prompt": nrm(ks[6], (BATCH, D_MODEL)),
        "c_sample": nrm(ks[7], (DEC_BATCH, D_MODEL)),
        "w_ada": nrm(ks[8], (DEPTH, D_MODEL, 6 * D_MODEL), 0.5 * D_MODEL ** -0.5),
        "b_ada": nrm(ks[9], (DEPTH, 6 * D_MODEL), 0.02),
        "norm_mix": 1.0 + nrm(ks[10], (DEPTH, D_MODEL), 0.05),
        "norm_ffn": 1.0 + nrm(ks[11], (DEPTH, D_MODEL), 0.05),
        "w_pool": nrm(ks[12], (N_POOL_LAYERS, N_POOL_GROUPS, POOL_GROUP_DIM, POOL_GROUP_DIM), POOL_GROUP_DIM ** -0.5),
        "pool_scale": 1.0 + nrm(ks[13], (N_POOL_LAYERS, D_MODEL), 0.1),
        "w_qkv": nrm(ks[14], (N_SB_LAYERS, D_MODEL, 3 * D_MODEL), D_MODEL ** -0.5),
        "q_norm": 1.0 + nrm(ks[15], (N_SB_LAYERS, HEAD_DIM), 0.05),
        "k_norm": 1.0 + nrm(ks[16], (N_SB_LAYERS, HEAD_DIM), 0.05),
        "sb_bias": SB_BIAS_INIT + nrm(ks[20], (N_SB_LAYERS, N_HEADS), 0.5),
        "w_o": nrm(ks[17], (N_SB_LAYERS, D_MODEL, D_MODEL), D_MODEL ** -0.5),
        "w_gu": nrm(ks[18], (DEPTH, D_MODEL, 2 * D_FF), D_MODEL ** -0.5),
        "w_down": nrm(ks[19], (DEPTH, D_FF, D_MODEL), D_FF ** -0.5),
    }


def reference(x_prompt, x_sample, state_pool, cache_k, cache_v, page_table, c_prompt, c_sample,
              w_ada, b_ada, norm_mix, norm_ffn, w_pool, pool_scale, w_qkv, q_norm, k_norm, sb_bias,
              w_o, w_gu, w_down):
    xp, xs = x_prompt, x_sample
    B, S, D = xp.shape
    DB, T, _ = xs.shape
    n_pages = page_table.shape[1]
    past = n_pages * PAGE_SIZE
    pool_p, pool_s, kp_l, vp_l, ks_l, vs_l = [], [], [], [], [], []
    for l in range(DEPTH):
        sh1p, sc1p, g1p, sh2p, sc2p, g2p = ada_modulation(c_prompt, w_ada[l], b_ada[l])
        sh1s, sc1s, g1s, sh2s, sc2s, g2s = ada_modulation(c_sample, w_ada[l], b_ada[l])
        hp = modulate(xp, norm_mix[l], sh1p, sc1p)
        hs = modulate(xs, norm_mix[l], sh1s, sc1s)
        if l % N_MIXERS == 0:
            j = l // N_MIXERS
            zeros_prefix = jnp.zeros((B, POOL_STATE_LEN, D), hp.dtype)
            op, st_p = pool_mix(hp, zeros_prefix, 0, w_pool[j], pool_scale[j])
            os_, st_s = pool_mix(hs, state_pool[j], past, w_pool[j], pool_scale[j])
            pool_p.append(st_p)
            pool_s.append(st_s)
        else:
            j = l // N_MIXERS
            qp, kp, vp = sb_qkv(hp, w_qkv[j], q_norm[j], k_norm[j])
            qs, kn, vn = sb_qkv(hs, w_qkv[j], q_norm[j], k_norm[j])
            op = sb_attend_prompt(qp, kp, vp, sb_bias[j]).reshape(B, S, D) @ w_o[j]
            k_past = cache_k[j][page_table].reshape(DB, past, N_HEADS, HEAD_DIM)
            v_past = cache_v[j][page_table].reshape(DB, past, N_HEADS, HEAD_DIM)
            os_ = sb_attend_sample(qs, kn, vn, k_past, v_past, sb_bias[j]).reshape(DB, T, D) @ w_o[j]
            kp_l.append(kp); vp_l.append(vp); ks_l.append(kn); vs_l.append(vn)
        xp = xp + g1p * op
        xs = xs + g1s * os_
        xp = xp + g2p * swiglu(modulate(xp, norm_ffn[l], sh2p, sc2p), w_gu[l], w_down[l])
        xs = xs + g2s * swiglu(modulate(xs, norm_ffn[l], sh2s, sc2s), w_gu[l], w_down[l])
    pool_state_prompt = jnp.stack(pool_p)
    pool_state_sample = jnp.stack(pool_s)
    k_prompt = jnp.stack(kp_l)
    v_prompt = jnp.stack(vp_l)
    k_sample = jnp.stack(ks_l)
    v_sample = jnp.stack(vs_l)
    return (xp, xs, pool_state_prompt, pool_state_sample, k_prompt, v_prompt, k_sample, v_sample)
```

```python
import functools

import jax
import jax.numpy as jnp
from jax import lax
from jax.experimental import pallas as pl
from jax.experimental.pallas import tpu as pltpu

POOL_WINDOWS = (2, 4, 8, 16)
POOL_HALO = 16
HEAD_DIM = 64
PAGE_SIZE = 128
EPS = 1e-6
F32 = jnp.float32
BF16 = jnp.bfloat16

V7X_LANES = 128
ATT_TILE = 256
PAGES_PER_STEP = 8
DEC_ROWS = 128
VMEM_LIMIT = 52 * 1024 * 1024


def _cparams(sem):
    return pltpu.CompilerParams(dimension_semantics=sem, vmem_limit_bytes=VMEM_LIMIT)


def _modulate(x, g, shift, scale):
    ms = jnp.mean(x * x, axis=-1, keepdims=True)
    y = x * lax.rsqrt(ms + EPS) * g
    return y * (1.0 + scale) + shift


def _softplus(z):
    return jnp.maximum(z, 0.0) + jnp.log(1.0 + jnp.exp(-jnp.abs(z)))


def _bdot(a, b):
    return jnp.dot(a, b, preferred_element_type=F32)


def _ada_kernel(c_ref, w_ref, b_ref, o_ref):
    c = c_ref[...]
    s = c * jax.nn.sigmoid(c)
    o_ref[0] = jnp.dot(s, w_ref[0], preferred_element_type=F32,
                       precision=lax.Precision.HIGHEST) + b_ref[0]


def _ada(c_all, w_ada, b_ada):
    depth, d, n = w_ada.shape
    rows = c_all.shape[0]
    tn = n // 4
    return pl.pallas_call(
        _ada_kernel,
        grid=(depth, n // tn),
        in_specs=[pl.BlockSpec((rows, d), lambda l, j: (0, 0)),
                  pl.BlockSpec((1, d, tn), lambda l, j: (l, 0, j)),
                  pl.BlockSpec((1, 1, tn), lambda l, j: (l, 0, j))],
        out_specs=pl.BlockSpec((1, rows, tn), lambda l, j: (l, 0, j)),
        out_shape=jax.ShapeDtypeStruct((depth, rows, n), F32),
        compiler_params=_cparams(("arbitrary", "arbitrary")),
        name="ada_mod",
    )(c_all, w_ada, b_ada.reshape(depth, 1, n))


def _pool_groups(h, lookback, cnt_of, wp_ref):
    gd = h.shape[1] // len(POOL_WINDOWS)
    outs = []
    for gi, w in enumerate(POOL_WINDOWS):
        c0 = gi * gd
        cur = h[:, c0:c0 + gd]
        s = cur
        for k in range(1, w):
            s = s + lookback(k, c0, gd)
        pooled = s / cnt_of(w) - cur
        outs.append(_bdot(pooled.astype(BF16), wp_ref[gi]))
    return jnp.concatenate(outs, axis=-1)


def _pool_prompt_kernel(x_ref, xprev_ref, sh_ref, sc_ref, gate_ref, g_ref, wp_ref, ps_ref,
                        o_ref, st_ref, ext_ref, *, tm):
    i = pl.program_id(0)
    g, sh, sc = g_ref[...], sh_ref[...], sc_ref[...]
    x = x_ref[...]
    h = _modulate(x, g, sh, sc)
    hp = _modulate(xprev_ref[...], g, sh, sc)
    ext_ref[0:POOL_HALO, :] = jnp.where(i > 0, hp, 0.0)
    ext_ref[POOL_HALO:POOL_HALO + tm, :] = h
    st_ref[...] = h[tm - POOL_HALO:, :]
    pos = i * tm + lax.broadcasted_iota(jnp.int32, (tm, 1), 0)

    def lookback(k, c0, gd):
        return ext_ref[POOL_HALO - k:POOL_HALO - k + tm, c0:c0 + gd]

    def cnt_of(w):
        return jnp.minimum(pos + 1, w).astype(F32)

    out = _pool_groups(h, lookback, cnt_of, wp_ref) * ps_ref[...]
    o_ref[...] = x + gate_ref[...] * out


def _pool_prompt(x, sh, sc, gate, g, wp, ps):
    m, d = x.shape
    tm = min(512, m)
    blocks_per_tile = tm // POOL_HALO
    row = lambda i: (0, 0)
    return pl.pallas_call(
        functools.partial(_pool_prompt_kernel, tm=tm),
        grid=(m // tm,),
        in_specs=[pl.BlockSpec((tm, d), lambda i: (i, 0)),
                  pl.BlockSpec((POOL_HALO, d), lambda i: (jnp.maximum(i * blocks_per_tile - 1, 0), 0)),
                  pl.BlockSpec((1, d), row), pl.BlockSpec((1, d), row), pl.BlockSpec((1, d), row),
                  pl.BlockSpec((1, d), row),
                  pl.BlockSpec(wp.shape, lambda i: (0, 0, 0)),
                  pl.BlockSpec((1, d), row)],
        out_specs=[pl.BlockSpec((tm, d), lambda i: (i, 0)),
                   pl.BlockSpec((POOL_HALO, d), row)],
        out_shape=[jax.ShapeDtypeStruct((m, d), F32),
                   jax.ShapeDtypeStruct((POOL_HALO, d), F32)],
        scratch_shapes=[pltpu.VMEM((POOL_HALO + tm, d), F32)],
        compiler_params=_cparams(("arbitrary",)),
        name="pool_prompt",
    )(x, x, sh, sc, gate, g, wp, ps)


def _pool_sample_kernel(x_ref, st_ref, sh_ref, sc_ref, gate_ref, g_ref, wp_ref, ps_ref,
                        o_ref, nst_ref, *, past):
    n_tok, n_state = x_ref.shape[0], st_ref.shape[0]
    g, sh, sc = g_ref[...], sh_ref[...], sc_ref[...]
    hs = [_modulate(x_ref[t], g, sh, sc) for t in range(n_tok)]
    ext = [st_ref[r] for r in range(n_state)] + hs
    for t in range(n_tok):
        lookback = lambda k, c0, gd, t=t: ext[n_state + t - k][:, c0:c0 + gd]
        cnt_of = lambda w, t=t: float(min(past + t + 1, w))
        out = _pool_groups(hs[t], lookback, cnt_of, wp_ref) * ps_ref[...]
        o_ref[t] = x_ref[t] + gate_ref[...] * out
    for r in range(n_state):
        nst_ref[r] = ext[n_tok + r]


def _pool_sample(x_tm, state_tm, sh, sc, gate, g, wp, ps, past):
    return pl.pallas_call(
        functools.partial(_pool_sample_kernel, past=past),
        out_shape=[jax.ShapeDtypeStruct(x_tm.shape, F32),
                   jax.ShapeDtypeStruct(state_tm.shape, F32)],
        compiler_params=pltpu.CompilerParams(vmem_limit_bytes=VMEM_LIMIT),
        name="pool_sample",
    )(x_tm, state_tm, sh, sc, gate, g, wp, ps)


def _ffn_kernel(x_ref, sh_ref, sc_ref, gate_ref, g_ref, wg_ref, wu_ref, wd_ref, o_ref,
                hb_ref, acc_ref):
    j = pl.program_id(1)

    @pl.when(j == 0)
    def _():
        hb_ref[...] = _modulate(x_ref[...], g_ref[...], sh_ref[...], sc_ref[...]).astype(BF16)
        acc_ref[...] = jnp.zeros_like(acc_ref)

    hb = hb_ref[...]
    gg = _bdot(hb, wg_ref[...])
    uu = _bdot(hb, wu_ref[...])
    act = (gg * jax.nn.sigmoid(gg)) * uu
    acc_ref[...] += _bdot(act.astype(BF16), wd_ref[...])

    @pl.when(j == pl.num_programs(1) - 1)
    def _():
        o_ref[...] = x_ref[...] + gate_ref[...] * acc_ref[...]


def _mod_spec(mod, tm, d):
    if mod.shape[0] == 1:
        return pl.BlockSpec((1, d), lambda i, *_: (0, 0))
    return pl.BlockSpec((tm, d), lambda i, *_: (i, 0))


def _ffn(x, sh, sc, gate, g, w_gu, w_down):
    m, d = x.shape
    f = w_down.shape[0]
    tm = min(1024, m)
    tf = 256 if f % 256 == 0 else f
    nf = f // tf
    return pl.pallas_call(
        _ffn_kernel,
        grid=(m // tm, nf),
        in_specs=[pl.BlockSpec((tm, d), lambda i, j: (i, 0)),
                  _mod_spec(sh, tm, d), _mod_spec(sc, tm, d), _mod_spec(gate, tm, d),
                  pl.BlockSpec((1, d), lambda i, j: (0, 0)),
                  pl.BlockSpec((d, tf), lambda i, j: (0, j)),
                  pl.BlockSpec((d, tf), lambda i, j: (0, nf + j)),
                  pl.BlockSpec((tf, d), lambda i, j: (j, 0))],
        out_specs=pl.BlockSpec((tm, d), lambda i, j: (i, 0)),
        out_shape=jax.ShapeDtypeStruct((m, d), F32),
        scratch_shapes=[pltpu.VMEM((tm, d), BF16), pltpu.VMEM((tm, d), F32)],
        compiler_params=_cparams(("arbitrary", "arbitrary")),
        name="ffn",
    )(x, sh, sc, gate, g, w_gu, w_gu, w_down)


def _split3(r):
    r1 = r.astype(BF16)
    e1 = r - r1.astype(F32)
    r2 = e1.astype(BF16)
    r3 = (e1 - r2.astype(F32)).astype(BF16)
    return r1, r2, r3


def _head_rms(q, seg, seg_t3, gain):
    sq = q * q
    s1, s2, s3 = _split3(sq)
    ssum = _bdot(s1, seg) + _bdot(s2, seg) + _bdot(s3, seg)
    r = lax.rsqrt(ssum * (1.0 / HEAD_DIM) + EPS)
    rb = _bdot(jnp.concatenate(_split3(r), axis=-1), seg_t3)
    return q * rb * gain


def _qkv_kernel(x_ref, sh_ref, sc_ref, g_ref, w_ref, seg_ref, segt_ref, gq_ref, gk_ref, *out_refs,
                attn_layout, scale):
    d = x_ref.shape[1]
    hb = _modulate(x_ref[...], g_ref[...], sh_ref[...], sc_ref[...]).astype(BF16)
    qkv = _bdot(hb, w_ref[...])
    seg, seg_t3 = seg_ref[...], segt_ref[...]
    qn = _head_rms(qkv[:, :d], seg, seg_t3, gq_ref[...])
    kn = _head_rms(qkv[:, d:2 * d], seg, seg_t3, gk_ref[...])
    v = qkv[:, 2 * d:]
    if not attn_layout:
        q_ref, k_ref, v_ref = out_refs
        q_ref[...] = qn
        k_ref[...] = kn
        v_ref[...] = v
        return
    k_ref, v_ref, kb_ref, qt_ref, vt_ref = out_refs
    k_ref[...] = kn
    v_ref[...] = v
    kb_ref[...] = kn.astype(BF16)
    qt_ref[...] = (qn * scale).T.astype(BF16)
    vt = v.T.astype(BF16)
    n_heads, n_chunks = vt_ref.shape[0], vt_ref.shape[1]
    for hh in range(n_heads):
        for c in range(n_chunks):
            vt_ref[hh, c] = vt[hh * HEAD_DIM:(hh + 1) * HEAD_DIM, c * ATT_TILE:(c + 1) * ATT_TILE]


def _qkv(x, sh, sc, g, w_qkv, seg, seg_t3, gq, gk, attn_layout):
    m, d = x.shape
    n_heads = d // HEAD_DIM
    tm = min(256, m)
    const = lambda i: (0, 0)
    rows = lambda i: (i, 0)
    if attn_layout:
        out_specs = [pl.BlockSpec((tm, d), rows), pl.BlockSpec((tm, d), rows),
                     pl.BlockSpec((tm, d), rows),
                     pl.BlockSpec((d, tm), lambda i: (0, i)),
                     pl.BlockSpec((n_heads, tm // ATT_TILE, HEAD_DIM, ATT_TILE), lambda i: (0, i, 0, 0))]
        out_shape = [jax.ShapeDtypeStruct((m, d), F32), jax.ShapeDtypeStruct((m, d), F32),
                     jax.ShapeDtypeStruct((m, d), BF16),
                     jax.ShapeDtypeStruct((d, m), BF16),
                     jax.ShapeDtypeStruct((n_heads, m // ATT_TILE, HEAD_DIM, ATT_TILE), BF16)]
    else:
        out_specs = [pl.BlockSpec((tm, d), rows)] * 3
        out_shape = [jax.ShapeDtypeStruct((m, d), F32)] * 3
    return pl.pallas_call(
        functools.partial(_qkv_kernel, attn_layout=attn_layout, scale=HEAD_DIM ** -0.5),
        grid=(m // tm,),
        in_specs=[pl.BlockSpec((tm, d), rows),
                  _mod_spec(sh, tm, d), _mod_spec(sc, tm, d),
                  pl.BlockSpec((1, d), const),
                  pl.BlockSpec(w_qkv.shape, const),
                  pl.BlockSpec(seg.shape, const), pl.BlockSpec(seg_t3.shape, const),
                  pl.BlockSpec((1, d), const), pl.BlockSpec((1, d), const)],
        out_specs=out_specs,
        out_shape=out_shape,
        compiler_params=_cparams(("arbitrary",)),
        name="qkv_attn" if attn_layout else "qkv_plain",
    )(x, sh, sc, g, w_qkv, seg, seg_t3, gq, gk)


def _attn_kernel(bias_ref, qt_ref, k_ref, vt_ref, ut_ref, o_ref, qaug_ref):
    t = ATT_TILE
    h, qi = pl.program_id(0), pl.program_id(1)
    bias = bias_ref[h]
    zeros = jnp.zeros((HEAD_DIM, t), BF16)

    @pl.when(h % 2 == 0)
    def _():
        qaug_ref[0:HEAD_DIM, :] = qt_ref[...]
        qaug_ref[HEAD_DIM:, :] = zeros

    @pl.when(h % 2 == 1)
    def _():
        qaug_ref[0:HEAD_DIM, :] = zeros
        qaug_ref[HEAD_DIM:, :] = qt_ref[...]

    qaug = qaug_ref[...]
    ut = ut_ref[...]

    def tile(kv, carry, acc, mask):
        kt = k_ref[pl.ds(pl.multiple_of(kv * t, t), t), :]
        z = _bdot(kt, qaug) + bias
        sp = _softplus(z)
        if mask is not None:
            sp = jnp.where(mask, sp, 0.0)
        cum_tot = _bdot(ut, sp.astype(BF16))
        a = jnp.exp(z - cum_tot[:t] - carry)
        if mask is not None:
            a = jnp.where(mask, a, 0.0)
        acc = acc + _bdot(vt_ref[0, kv], a.astype(BF16))
        return carry + cum_tot[t:t + 1], acc

    key_pos = lax.broadcasted_iota(jnp.int32, (t, t), 0)
    qry_pos = lax.broadcasted_iota(jnp.int32, (t, t), 1)
    carry, acc = tile(qi, jnp.zeros((1, t), F32), jnp.zeros((HEAD_DIM, t), F32), key_pos < qry_pos)

    def body(j, state):
        return tile(qi - 1 - j, state[0], state[1], None)

    carry, acc = lax.fori_loop(0, qi, body, (carry, acc))
    o_ref[...] = acc


def _attention_prompt(sb_bias, q_t, kb, v_t, ut):
    d, s = q_t.shape
    n_heads = d // HEAD_DIM
    t = ATT_TILE
    grid_spec = pltpu.PrefetchScalarGridSpec(
        num_scalar_prefetch=1,
        grid=(n_heads, s // t),
        in_specs=[pl.BlockSpec((HEAD_DIM, t), lambda h, qi, b: (h, qi)),
                  pl.BlockSpec((s, 2 * HEAD_DIM), lambda h, qi, b: (0, h // 2)),
                  pl.BlockSpec((1, s // t, HEAD_DIM, t), lambda h, qi, b: (h, 0, 0, 0)),
                  pl.BlockSpec(ut.shape, lambda h, qi, b: (0, 0))],
        out_specs=pl.BlockSpec((HEAD_DIM, t), lambda h, qi, b: (h, qi)),
        scratch_shapes=[pltpu.VMEM((2 * HEAD_DIM, t), BF16)],
    )
    return pl.pallas_call(
        _attn_kernel,
        grid_spec=grid_spec,
        out_shape=jax.ShapeDtypeStruct((d, s), F32),
        compiler_params=_cparams(("arbitrary", "arbitrary")),
        name="sb_attn_prompt",
    )(sb_bias, q_t, kb, v_t, ut)


def _wo_kernel(o_ref, x_ref, gate_ref, w_ref, out_ref, *, transposed):
    o = o_ref[...]
    if transposed:
        o = o.T
    out_ref[...] = x_ref[...] + gate_ref[...] * _bdot(o.astype(BF16), w_ref[...])


def _wo(o, x, gate, w_o, transposed):
    m, d = x.shape
    tm = min(512, m)
    o_spec = pl.BlockSpec((d, tm), lambda i: (0, i)) if transposed else pl.BlockSpec((tm, d), lambda i: (i, 0))
    return pl.pallas_call(
        functools.partial(_wo_kernel, transposed=transposed),
        grid=(m // tm,),
        in_specs=[o_spec, pl.BlockSpec((tm, d), lambda i: (i, 0)), _mod_spec(gate, tm, d),
                  pl.BlockSpec((d, d), lambda i: (0, 0))],
        out_specs=pl.BlockSpec((tm, d), lambda i: (i, 0)),
        out_shape=jax.ShapeDtypeStruct((m, d), F32),
        compiler_params=_cparams(("arbitrary",)),
        name="wo_t" if transposed else "wo",
    )(o, x, gate, w_o)


def _decode_kernel(pt_ref, qbd_ref, bias_ref, kn_ref, vn_ref, u_ref, *refs, n_tok, n_heads):
    p = PAGES_PER_STEP
    k_refs, v_refs = refs[:p], refs[p:2 * p]
    o_ref, carry_ref, acc_ref = refs[2 * p:]
    c = pl.program_id(1)
    qbd = qbd_ref[0]
    bias = bias_ref[...]
    u = u_ref[...]
    nt = (((1,), (1,)), ((), ()))

    @pl.when(c == 0)
    def _():
        z = lax.dot_general(qbd, kn_ref[0].astype(BF16), nt, preferred_element_type=F32) + bias
        row = lax.broadcasted_iota(jnp.int32, z.shape, 0)
        col = lax.broadcasted_iota(jnp.int32, z.shape, 1)
        mask = (col < row // n_heads) & (col < n_tok)
        sp = jnp.where(mask, _softplus(z), 0.0)
        cum = _bdot(sp.astype(BF16), u[:z.shape[1], :z.shape[1]])
        a = jnp.where(mask, jnp.exp(z - cum), 0.0)
        acc_ref[...] = _bdot(a.astype(BF16), vn_ref[0].astype(BF16))
        carry_ref[...] = jnp.broadcast_to(cum[:, 0:1], carry_ref.shape)

    carry = carry_ref[:, 0:1]
    acc = acc_ref[...]
    for gi in reversed(range(p // 2)):
        kk = jnp.concatenate([k_refs[2 * gi][0], k_refs[2 * gi + 1][0]], axis=0).astype(BF16)
        vv = jnp.concatenate([v_refs[2 * gi][0], v_refs[2 * gi + 1][0]], axis=0).astype(BF16)
        z = lax.dot_general(qbd, kk, nt, preferred_element_type=F32) + bias
        cum = _bdot(_softplus(z).astype(BF16), u)
        a = jnp.exp(z - cum - carry)
        acc = acc + _bdot(a.astype(BF16), vv)
        carry = carry + cum[:, 0:1]
    acc_ref[...] = acc
    carry_ref[...] = jnp.broadcast_to(carry, carry_ref.shape)

    @pl.when(c == pl.num_programs(1) - 1)
    def _():
        row = lax.broadcasted_iota(jnp.int32, acc.shape, 0)
        col = lax.broadcasted_iota(jnp.int32, acc.shape, 1)
        own = jnp.where(row % n_heads == col // HEAD_DIM, acc, 0.0)
        for t in range(n_tok):
            o_ref[0, t:t + 1, :] = jnp.sum(own[t * n_heads:(t + 1) * n_heads], axis=0, keepdims=True)


def _attention_sample(page_table, qbd, bias_col, kn_pad, vn_pad, u, cache_k, cache_v, n_tok):
    db, n_pages = page_table.shape
    d = cache_k.shape[-1]
    n_heads = d // HEAD_DIM
    p = PAGES_PER_STEP
    n_steps = n_pages // p

    def page_spec(i):
        return pl.BlockSpec((1, PAGE_SIZE, d),
                            lambda b, c, pt: (pt[b, (n_steps - 1 - c) * p + i], 0, 0))

    per_seq = lambda b, c, pt: (b, 0, 0)
    grid_spec = pltpu.PrefetchScalarGridSpec(
        num_scalar_prefetch=1,
        grid=(db, n_steps),
        in_specs=[pl.BlockSpec((1, DEC_ROWS, d), per_seq),
                  pl.BlockSpec(bias_col.shape, lambda b, c, pt: (0, 0)),
                  pl.BlockSpec((1, PAGE_SIZE, d), per_seq),
                  pl.BlockSpec((1, PAGE_SIZE, d), per_seq),
                  pl.BlockSpec(u.shape, lambda b, c, pt: (0, 0))]
                 + [page_spec(i) for i in range(p)] * 2,
        out_specs=pl.BlockSpec((1, n_tok, d), per_seq),
        scratch_shapes=[pltpu.VMEM((DEC_ROWS, V7X_LANES), F32), pltpu.VMEM((DEC_ROWS, d), F32)],
    )
    return pl.pallas_call(
        functools.partial(_decode_kernel, n_tok=n_tok, n_heads=n_heads),
        grid_spec=grid_spec,
        out_shape=jax.ShapeDtypeStruct((db, n_tok, d), F32),
        compiler_params=_cparams(("arbitrary", "arbitrary")),
        name="sb_attn_decode",
    )(page_table, qbd, bias_col, kn_pad, vn_pad, u, *([cache_k] * p), *([cache_v] * p))


def _suffix_sum_matrix(n, extra_ones_rows=0):
    r = lax.broadcasted_iota(jnp.int32, (n + extra_ones_rows, n), 0)
    c = lax.broadcasted_iota(jnp.int32, (n + extra_ones_rows, n), 1)
    return ((c >= r) | (r >= n)).astype(BF16)


def kernel(x_prompt, x_sample, state_pool, cache_k, cache_v, page_table, c_prompt, c_sample, w_ada, b_ada, norm_mix, norm_ffn, w_pool, pool_scale, w_qkv, q_norm, k_norm, sb_bias, w_o, w_gu, w_down):
    batch, seq, d = x_prompt.shape
    db, n_tok, _ = x_sample.shape
    depth = w_ada.shape[0]
    n_heads = d // HEAD_DIM
    n_pages = page_table.shape[1]
    past = n_pages * PAGE_SIZE
    assert batch == 1 and seq % 1024 == 0 and d % (2 * V7X_LANES) == 0
    assert n_pages % PAGES_PER_STEP == 0 and n_tok * n_heads <= DEC_ROWS and n_tok <= PAGE_SIZE
    assert (db * n_tok) % 16 == 0 and state_pool.shape[2] == POOL_HALO - 1

    c_all = jnp.concatenate([c_prompt, c_sample], axis=0)
    pad = (-c_all.shape[0]) % 8
    c_all = jnp.pad(c_all, ((0, pad), (0, 0)))
    mods = _ada(c_all, w_ada, b_ada)

    xp = x_prompt.reshape(seq, d)
    xs = x_sample.transpose(1, 0, 2).reshape(n_tok * db, d)

    seg = (lax.broadcasted_iota(jnp.int32, (d, V7X_LANES), 0) // HEAD_DIM
           == lax.broadcasted_iota(jnp.int32, (d, V7X_LANES), 1)).astype(BF16)
    seg_t3 = jnp.tile(seg.T, (3, 1))
    ut = _suffix_sum_matrix(ATT_TILE, 8)
    u = _suffix_sum_matrix(ATT_TILE).T

    w_gu_b, w_down_b = w_gu.astype(BF16), w_down.astype(BF16)
    w_pool_b, w_qkv_b, w_o_b = w_pool.astype(BF16), w_qkv.astype(BF16), w_o.astype(BF16)

    pool_p, pool_s, kp_l, vp_l, ks_l, vs_l = [], [], [], [], [], []
    for l in range(depth):
        m6 = mods[l].reshape(-1, 6, d)
        mp = [m6[0:1, i] for i in range(6)]
        ms = [m6[1:1 + db, i] for i in range(6)]
        ms_rows = [jnp.tile(a, (n_tok, 1)) for a in ms]
        g_mix, g_ffn = norm_mix[l][None, :], norm_ffn[l][None, :]
        j = l // 2
        if l % 2 == 0:
            xp, st_p = _pool_prompt(xp, mp[0], mp[1], mp[2], g_mix, w_pool_b[j], pool_scale[j][None, :])
            xs3, st_s = _pool_sample(xs.reshape(n_tok, db, d), state_pool[j].transpose(1, 0, 2),
                                     ms[0], ms[1], ms[2], g_mix, w_pool_b[j], pool_scale[j][None, :], past)
            xs = xs3.reshape(n_tok * db, d)
            pool_p.append(st_p[1:][None])
            pool_s.append(st_s.transpose(1, 0, 2))
        else:
            gq = jnp.tile(q_norm[j], n_heads)[None, :]
            gk = jnp.tile(k_norm[j], n_heads)[None, :]
            kp, vp, kb, q_t, v_t = _qkv(xp, mp[0], mp[1], g_mix, w_qkv_b[j], seg, seg_t3, gq, gk, True)
            o_t = _attention_prompt(sb_bias[j], q_t, kb, v_t, ut)
            xp = _wo(o_t, xp, mp[2], w_o_b[j], True)

            qs, kn, vn = _qkv(xs, ms_rows[0], ms_rows[1], g_mix, w_qkv_b[j], seg, seg_t3, gq, gk, False)
            to_seq = lambda a: a.reshape(n_tok, db, d).transpose(1, 0, 2)
            q_seq = (to_seq(qs) * HEAD_DIM ** -0.5).reshape(db, n_tok, n_heads, HEAD_DIM)
            qbd = jnp.einsum('bthd,hg->bthgd', q_seq, jnp.eye(n_heads, dtype=F32))
            qbd = qbd.reshape(db, n_tok * n_heads, d)
            qbd = jnp.pad(qbd, ((0, 0), (0, DEC_ROWS - n_tok * n_heads), (0, 0))).astype(BF16)
            bias_col = jnp.tile(sb_bias[j], DEC_ROWS // n_heads)[:, None]
            pad_keys = lambda a: jnp.pad(to_seq(a), ((0, 0), (0, PAGE_SIZE - n_tok), (0, 0)))
            n_phys = cache_k.shape[1]
            o_s = _attention_sample(page_table, qbd, bias_col, pad_keys(kn), pad_keys(vn), u,
                                    cache_k[j].reshape(n_phys, PAGE_SIZE, d),
                                    cache_v[j].reshape(n_phys, PAGE_SIZE, d), n_tok)
            xs = _wo(o_s.transpose(1, 0, 2).reshape(n_tok * db, d), xs, ms_rows[2], w_o_b[j], False)

            kp_l.append(kp.reshape(batch, seq, n_heads, HEAD_DIM))
            vp_l.append(vp.reshape(batch, seq, n_heads, HEAD_DIM))
            ks_l.append(to_seq(kn).reshape(db, n_tok, n_heads, HEAD_DIM))
            vs_l.append(to_seq(vn).reshape(db, n_tok, n_heads, HEAD_DIM))
        xp = _ffn(xp, mp[3], mp[4], mp[5], g_ffn, w_gu_b[l], w_down_b[l])
        xs = _ffn(xs, ms_rows[3], ms_rows[4], ms_rows[5], g_ffn, w_gu_b[l], w_down_b[l])

    y_sample = xs.reshape(n_tok, db, d).transpose(1, 0, 2)
    return (xp.reshape(batch, seq, d), y_sample,
            jnp.stack(pool_p), jnp.stack(pool_s),
            jnp.stack(kp_l), jnp.stack(vp_l), jnp.stack(ks_l), jnp.stack(vs_l))
```

```python
import functools

import jax
import jax.numpy as jnp
from jax import lax
from jax.experimental import pallas as pl
from jax.experimental.pallas import tpu as pltpu

POOL_WINDOWS = (2, 4, 8, 16)
POOL_HALO = 16
HEAD_DIM = 64
PAGE_SIZE = 128
EPS = 1e-6
F32 = jnp.float32
BF16 = jnp.bfloat16

V7X_LANES = 128
ATT_TILE = 256
ATT_TQ = 512
LOG2E = 1.4426950408889634
BIAS_PIECES = 3
PAGES_PER_STEP = 8
DEC_ROWS = 128
VMEM_LIMIT = 52 * 1024 * 1024


def _cparams(sem):
    return pltpu.CompilerParams(dimension_semantics=sem, vmem_limit_bytes=VMEM_LIMIT)


def _modulate(x, g, shift, scale):
    ms = jnp.mean(x * x, axis=-1, keepdims=True)
    y = x * lax.rsqrt(ms + EPS) * g
    return y * (1.0 + scale) + shift


def _softplus(z):
    return jnp.maximum(z, 0.0) + jnp.log(1.0 + jnp.exp(-jnp.abs(z)))


def _bdot(a, b):
    return jnp.dot(a, b, preferred_element_type=F32)


def _ada_kernel(c_ref, w_ref, b_ref, o_ref):
    c = c_ref[...]
    s = c * jax.nn.sigmoid(c)
    o_ref[0] = jnp.dot(s, w_ref[0], preferred_element_type=F32,
                       precision=lax.Precision.HIGHEST) + b_ref[0]


def _ada(c_all, w_ada, b_ada):
    depth, d, n = w_ada.shape
    rows = c_all.shape[0]
    tn = n // 4
    return pl.pallas_call(
        _ada_kernel,
        grid=(depth, n // tn),
        in_specs=[pl.BlockSpec((rows, d), lambda l, j: (0, 0)),
                  pl.BlockSpec((1, d, tn), lambda l, j: (l, 0, j)),
                  pl.BlockSpec((1, 1, tn), lambda l, j: (l, 0, j))],
        out_specs=pl.BlockSpec((1, rows, tn), lambda l, j: (l, 0, j)),
        out_shape=jax.ShapeDtypeStruct((depth, rows, n), F32),
        compiler_params=_cparams(("arbitrary", "arbitrary")),
        name="ada_mod",
    )(c_all, w_ada, b_ada.reshape(depth, 1, n))


def _pool_groups(h, lookback, cnt_of, wp_ref):
    gd = h.shape[1] // len(POOL_WINDOWS)
    outs = []
    for gi, w in enumerate(POOL_WINDOWS):
        c0 = gi * gd
        cur = h[:, c0:c0 + gd]
        s = cur
        for k in range(1, w):
            s = s + lookback(k, c0, gd)
        pooled = s / cnt_of(w) - cur
        outs.append(_bdot(pooled.astype(BF16), wp_ref[gi]))
    return jnp.concatenate(outs, axis=-1)


def _pool_prompt_kernel(x_ref, xprev_ref, sh_ref, sc_ref, gate_ref, g_ref, wp_ref, ps_ref,
                        o_ref, st_ref, ext_ref, *, tm):
    i = pl.program_id(0)
    g, sh, sc = g_ref[...], sh_ref[...], sc_ref[...]
    x = x_ref[...]
    h = _modulate(x, g, sh, sc)
    hp = _modulate(xprev_ref[...], g, sh, sc)
    ext_ref[0:POOL_HALO, :] = jnp.where(i > 0, hp, 0.0)
    ext_ref[POOL_HALO:POOL_HALO + tm, :] = h
    st_ref[...] = h[tm - POOL_HALO:, :]
    pos = i * tm + lax.broadcasted_iota(jnp.int32, (tm, 1), 0)

    def lookback(k, c0, gd):
        return ext_ref[POOL_HALO - k:POOL_HALO - k + tm, c0:c0 + gd]

    def cnt_of(w):
        return jnp.minimum(pos + 1, w).astype(F32)

    out = _pool_groups(h, lookback, cnt_of, wp_ref) * ps_ref[...]
    o_ref[...] = x + gate_ref[...] * out


def _pool_prompt(x, sh, sc, gate, g, wp, ps):
    m, d = x.shape
    tm = min(512, m)
    blocks_per_tile = tm // POOL_HALO
    row = lambda i: (0, 0)
    return pl.pallas_call(
        functools.partial(_pool_prompt_kernel, tm=tm),
        grid=(m // tm,),
        in_specs=[pl.BlockSpec((tm, d), lambda i: (i, 0)),
                  pl.BlockSpec((POOL_HALO, d), lambda i: (jnp.maximum(i * blocks_per_tile - 1, 0), 0)),
                  pl.BlockSpec((1, d), row), pl.BlockSpec((1, d), row), pl.BlockSpec((1, d), row),
                  pl.BlockSpec((1, d), row),
                  pl.BlockSpec(wp.shape, lambda i: (0, 0, 0)),
                  pl.BlockSpec((1, d), row)],
        out_specs=[pl.BlockSpec((tm, d), lambda i: (i, 0)),
                   pl.BlockSpec((POOL_HALO, d), row)],
        out_shape=[jax.ShapeDtypeStruct((m, d), F32),
                   jax.ShapeDtypeStruct((POOL_HALO, d), F32)],
        scratch_shapes=[pltpu.VMEM((POOL_HALO + tm, d), F32)],
        compiler_params=_cparams(("arbitrary",)),
        name="pool_prompt",
    )(x, x, sh, sc, gate, g, wp, ps)


def _pool_sample_kernel(x_ref, st_ref, sh_ref, sc_ref, gate_ref, g_ref, wp_ref, ps_ref,
                        o_ref, nst_ref, *, past):
    n_tok, n_state = x_ref.shape[0], st_ref.shape[0]
    g, sh, sc = g_ref[...], sh_ref[...], sc_ref[...]
    hs = [_modulate(x_ref[t], g, sh, sc) for t in range(n_tok)]
    ext = [st_ref[r] for r in range(n_state)] + hs
    for t in range(n_tok):
        lookback = lambda k, c0, gd, t=t: ext[n_state + t - k][:, c0:c0 + gd]
        cnt_of = lambda w, t=t: float(min(past + t + 1, w))
        out = _pool_groups(hs[t], lookback, cnt_of, wp_ref) * ps_ref[...]
        o_ref[t] = x_ref[t] + gate_ref[...] * out
    for r in range(n_state):
        nst_ref[r] = ext[n_tok + r]


def _pool_sample(x_tm, state_tm, sh, sc, gate, g, wp, ps, past):
    return pl.pallas_call(
        functools.partial(_pool_sample_kernel, past=past),
        out_shape=[jax.ShapeDtypeStruct(x_tm.shape, F32),
                   jax.ShapeDtypeStruct(state_tm.shape, F32)],
        compiler_params=pltpu.CompilerParams(vmem_limit_bytes=VMEM_LIMIT),
        name="pool_sample",
    )(x_tm, state_tm, sh, sc, gate, g, wp, ps)


def _ffn_kernel(x_ref, sh_ref, sc_ref, gate_ref, g_ref, wg_ref, wu_ref, wd_ref, o_ref,
                hb_ref, acc_ref):
    j = pl.program_id(1)

    @pl.when(j == 0)
    def _():
        hb_ref[...] = _modulate(x_ref[...], g_ref[...], sh_ref[...], sc_ref[...]).astype(BF16)
        acc_ref[...] = jnp.zeros_like(acc_ref)

    hb = hb_ref[...]
    gg = _bdot(hb, wg_ref[...])
    uu = _bdot(hb, wu_ref[...])
    act = (gg * jax.nn.sigmoid(gg)) * uu
    acc_ref[...] += _bdot(act.astype(BF16), wd_ref[...])

    @pl.when(j == pl.num_programs(1) - 1)
    def _():
        o_ref[...] = x_ref[...] + gate_ref[...] * acc_ref[...]


def _mod_spec(mod, tm, d):
    if mod.shape[0] == 1:
        return pl.BlockSpec((1, d), lambda i, *_: (0, 0))
    return pl.BlockSpec((tm, d), lambda i, *_: (i, 0))


def _ffn(x, sh, sc, gate, g, w_gu, w_down):
    m, d = x.shape
    f = w_down.shape[0]
    tm = min(1024, m)
    tf = 256 if f % 256 == 0 else f
    nf = f // tf
    return pl.pallas_call(
        _ffn_kernel,
        grid=(m // tm, nf),
        in_specs=[pl.BlockSpec((tm, d), lambda i, j: (i, 0)),
                  _mod_spec(sh, tm, d), _mod_spec(sc, tm, d), _mod_spec(gate, tm, d),
                  pl.BlockSpec((1, d), lambda i, j: (0, 0)),
                  pl.BlockSpec((d, tf), lambda i, j: (0, j)),
                  pl.BlockSpec((d, tf), lambda i, j: (0, nf + j)),
                  pl.BlockSpec((tf, d), lambda i, j: (j, 0))],
        out_specs=pl.BlockSpec((tm, d), lambda i, j: (i, 0)),
        out_shape=jax.ShapeDtypeStruct((m, d), F32),
        scratch_shapes=[pltpu.VMEM((tm, d), BF16), pltpu.VMEM((tm, d), F32)],
        compiler_params=_cparams(("arbitrary", "arbitrary")),
        name="ffn",
    )(x, sh, sc, gate, g, w_gu, w_gu, w_down)


def _split3(r):
    r1 = r.astype(BF16)
    e1 = r - r1.astype(F32)
    r2 = e1.astype(BF16)
    r3 = (e1 - r2.astype(F32)).astype(BF16)
    return r1, r2, r3


def _head_rms(q, seg, seg_t3, gain):
    sq = q * q
    s1, s2, s3 = _split3(sq)
    ssum = _bdot(s1, seg) + _bdot(s2, seg) + _bdot(s3, seg)
    r = lax.rsqrt(ssum * (1.0 / HEAD_DIM) + EPS)
    rb = _bdot(jnp.concatenate(_split3(r), axis=-1), seg_t3)
    return q * rb * gain


def _qkv_kernel(x_ref, sh_ref, sc_ref, g_ref, w_ref, seg_ref, segt_ref, gq_ref, gk_ref, *out_refs,
                attn_layout, scale):
    d = x_ref.shape[1]
    hb = _modulate(x_ref[...], g_ref[...], sh_ref[...], sc_ref[...]).astype(BF16)
    qkv = _bdot(hb, w_ref[...])
    seg, seg_t3 = seg_ref[...], segt_ref[...]
    qn = _head_rms(qkv[:, :d], seg, seg_t3, gq_ref[...])
    kn = _head_rms(qkv[:, d:2 * d], seg, seg_t3, gk_ref[...])
    v = qkv[:, 2 * d:]
    if not attn_layout:
        q_ref, k_ref, v_ref = out_refs
        q_ref[...] = qn
        k_ref[...] = kn
        v_ref[...] = v
        return
    k_ref, v_ref, kb_ref, qt_ref, vt_ref = out_refs
    k_ref[...] = kn
    v_ref[...] = v
    lane = lax.broadcasted_iota(jnp.int32, (1, 2 * HEAD_DIM), 1)
    ones_cols = ((lane >= HEAD_DIM) & (lane < HEAD_DIM + BIAS_PIECES)).astype(F32)
    for hh in range(d // HEAD_DIM):
        pair = kn[:, (hh // 2) * 2 * HEAD_DIM:(hh // 2 + 1) * 2 * HEAD_DIM]
        if hh % 2:
            pair = pltpu.roll(pair, HEAD_DIM, 1)
        kb_ref[:, hh * 2 * HEAD_DIM:(hh + 1) * 2 * HEAD_DIM] = jnp.where(lane < HEAD_DIM, pair, ones_cols).astype(BF16)
    qt_ref[...] = (qn * scale).T.astype(BF16)
    vt = v.T.astype(BF16)
    n_heads, n_chunks = vt_ref.shape[0], vt_ref.shape[1]
    for hh in range(n_heads):
        for c in range(n_chunks):
            vt_ref[hh, c] = vt[hh * HEAD_DIM:(hh + 1) * HEAD_DIM, c * ATT_TILE:(c + 1) * ATT_TILE]


def _qkv(x, sh, sc, g, w_qkv, seg, seg_t3, gq, gk, attn_layout):
    m, d = x.shape
    n_heads = d // HEAD_DIM
    tm = min(256, m)
    const = lambda i: (0, 0)
    rows = lambda i: (i, 0)
    if attn_layout:
        out_specs = [pl.BlockSpec((tm, d), rows), pl.BlockSpec((tm, d), rows),
                     pl.BlockSpec((tm, 2 * d), rows),
                     pl.BlockSpec((d, tm), lambda i: (0, i)),
                     pl.BlockSpec((n_heads, tm // ATT_TILE, HEAD_DIM, ATT_TILE), lambda i: (0, i, 0, 0))]
        out_shape = [jax.ShapeDtypeStruct((m, d), F32), jax.ShapeDtypeStruct((m, d), F32),
                     jax.ShapeDtypeStruct((m, 2 * d), BF16),
                     jax.ShapeDtypeStruct((d, m), BF16),
                     jax.ShapeDtypeStruct((n_heads, m // ATT_TILE, HEAD_DIM, ATT_TILE), BF16)]
    else:
        out_specs = [pl.BlockSpec((tm, d), rows)] * 3
        out_shape = [jax.ShapeDtypeStruct((m, d), F32)] * 3
    return pl.pallas_call(
        functools.partial(_qkv_kernel, attn_layout=attn_layout, scale=HEAD_DIM ** -0.5 * LOG2E),
        grid=(m // tm,),
        in_specs=[pl.BlockSpec((tm, d), rows),
                  _mod_spec(sh, tm, d), _mod_spec(sc, tm, d),
                  pl.BlockSpec((1, d), const),
                  pl.BlockSpec(w_qkv.shape, const),
                  pl.BlockSpec(seg.shape, const), pl.BlockSpec(seg_t3.shape, const),
                  pl.BlockSpec((1, d), const), pl.BlockSpec((1, d), const)],
        out_specs=out_specs,
        out_shape=out_shape,
        compiler_params=_cparams(("arbitrary",)),
        name="qkv_attn" if attn_layout else "qkv_plain",
    )(x, sh, sc, g, w_qkv, seg, seg_t3, gq, gk)


def _neg_abs(z):
    sign = jnp.uint32(0x80000000)
    return lax.bitcast_convert_type(lax.bitcast_convert_type(z, jnp.uint32) | sign, F32)


def _softplus2(z):
    return jnp.maximum(z, 0.0) + jnp.log2(1.0 + jnp.exp2(_neg_abs(z)))


def _attn_kernel(bias_ref, qt_ref, k_ref, vt_ref, ut_ref, o_ref, qaug_ref, acc_ref, carry_ref):
    t, tq = ATT_TILE, ATT_TQ
    halves = tq // t
    chunks = [(c // halves, c % halves, slice(c * t, (c + 1) * t)) for c in range(2 * halves)]
    p, qi = pl.program_id(0), pl.program_id(1)
    row = lax.broadcasted_iota(jnp.int32, (HEAD_DIM, tq), 0)
    for head in range(2):
        b = jnp.full((HEAD_DIM, tq), bias_ref[2 * p + head], F32)
        pieces = _split3(b)
        aug = jnp.zeros((HEAD_DIM, tq), F32)
        for i in reversed(range(BIAS_PIECES)):
            aug = jnp.where(row == i, pieces[i].astype(F32), aug)
        qaug_ref[0:HEAD_DIM, head * tq:(head + 1) * tq] = qt_ref[head * HEAD_DIM:(head + 1) * HEAD_DIM, :]
        qaug_ref[HEAD_DIM:, head * tq:(head + 1) * tq] = aug.astype(BF16)
    acc_ref[...] = jnp.zeros_like(acc_ref)
    carry_ref[...] = jnp.zeros_like(carry_ref)
    ut = ut_ref[...]
    key_pos = lax.broadcasted_iota(jnp.int32, (t, t), 0)
    qry_pos = lax.broadcasted_iota(jnp.int32, (t, t), 1)

    def tile(kv, masked):
        kt = k_ref[pl.ds(pl.multiple_of(kv * t, t), t), :]
        zs = [_bdot(kt[:, head * 2 * HEAD_DIM:(head + 1) * 2 * HEAD_DIM], qaug_ref[:, cols])
              for head, _, cols in chunks]
        masks = [key_pos + (kv * t - qi * tq - half * t) < qry_pos if masked else None
                 for _, half, _ in chunks]
        sps = []
        for z, mask in zip(zs, masks):
            sp = _softplus2(z)
            if masked:
                sp = jnp.where(mask, sp, 0.0)
            sps.append(sp.astype(BF16))
        cum_tots = [_bdot(ut, sp) for sp in sps]
        for (head, half, cols), z, mask, cum_tot in zip(chunks, zs, masks, cum_tots):
            carry = carry_ref[0:1, cols]
            a = jnp.exp2(z - cum_tot[:t])
            if masked:
                a = jnp.where(mask, a, 0.0)
            rows = slice(head * HEAD_DIM, (head + 1) * HEAD_DIM)
            acc_ref[rows, half * t:(half + 1) * t] += _bdot(vt_ref[head, kv], a.astype(BF16)) * jnp.exp2(-carry)
            carry_ref[0:1, cols] = carry + cum_tot[t:t + 1]

    for i in reversed(range(halves)):
        tile(qi * halves + i, True)

    def body(j, _):
        tile(qi * halves - 1 - j, False)
        return 0

    lax.fori_loop(0, qi * halves, body, 0)
    o_ref[...] = acc_ref[...]


def _attention_prompt(bias2, q_t, kb, v_t, ut):
    d, s = q_t.shape
    n_pairs = d // (2 * HEAD_DIM)
    t, tq = ATT_TILE, ATT_TQ
    grid_spec = pltpu.PrefetchScalarGridSpec(
        num_scalar_prefetch=1,
        grid=(n_pairs, s // tq),
        in_specs=[pl.BlockSpec((2 * HEAD_DIM, tq), lambda p, qi, b: (p, qi)),
                  pl.BlockSpec((s, 4 * HEAD_DIM), lambda p, qi, b: (0, p)),
                  pl.BlockSpec((2, s // t, HEAD_DIM, t), lambda p, qi, b: (p, 0, 0, 0)),
                  pl.BlockSpec(ut.shape, lambda p, qi, b: (0, 0))],
        out_specs=pl.BlockSpec((2 * HEAD_DIM, tq), lambda p, qi, b: (p, qi)),
        scratch_shapes=[pltpu.VMEM((2 * HEAD_DIM, 2 * tq), BF16),
                        pltpu.VMEM((2 * HEAD_DIM, tq), F32),
                        pltpu.VMEM((8, 2 * tq), F32)],
    )
    return pl.pallas_call(
        _attn_kernel,
        grid_spec=grid_spec,
        out_shape=jax.ShapeDtypeStruct((d, s), F32),
        compiler_params=_cparams(("arbitrary", "arbitrary")),
        name="sb_attn_prompt",
    )(bias2, q_t, kb, v_t, ut)


def _wo_kernel(o_ref, x_ref, gate_ref, w_ref, out_ref, *, transposed):
    o = o_ref[...]
    if transposed:
        o = o.T
    out_ref[...] = x_ref[...] + gate_ref[...] * _bdot(o.astype(BF16), w_ref[...])


def _wo(o, x, gate, w_o, transposed):
    m, d = x.shape
    tm = min(512, m)
    o_spec = pl.BlockSpec((d, tm), lambda i: (0, i)) if transposed else pl.BlockSpec((tm, d), lambda i: (i, 0))
    return pl.pallas_call(
        functools.partial(_wo_kernel, transposed=transposed),
        grid=(m // tm,),
        in_specs=[o_spec, pl.BlockSpec((tm, d), lambda i: (i, 0)), _mod_spec(gate, tm, d),
                  pl.BlockSpec((d, d), lambda i: (0, 0))],
        out_specs=pl.BlockSpec((tm, d), lambda i: (i, 0)),
        out_shape=jax.ShapeDtypeStruct((m, d), F32),
        compiler_params=_cparams(("arbitrary",)),
        name="wo_t" if transposed else "wo",
    )(o, x, gate, w_o)


def _decode_kernel(pt_ref, qbd_ref, bias_ref, kn_ref, vn_ref, u_ref, *refs, n_tok, n_heads):
    p = PAGES_PER_STEP
    k_refs, v_refs = refs[:p], refs[p:2 * p]
    o_ref, carry_ref, acc_ref = refs[2 * p:]
    c = pl.program_id(1)
    qbd = qbd_ref[0]
    bias = bias_ref[...]
    u = u_ref[...]
    nt = (((1,), (1,)), ((), ()))

    @pl.when(c == 0)
    def _():
        z = lax.dot_general(qbd, kn_ref[0].astype(BF16), nt, preferred_element_type=F32) + bias
        row = lax.broadcasted_iota(jnp.int32, z.shape, 0)
        col = lax.broadcasted_iota(jnp.int32, z.shape, 1)
        mask = (col < row // n_heads) & (col < n_tok)
        sp = jnp.where(mask, _softplus(z), 0.0)
        cum = _bdot(sp.astype(BF16), u[:z.shape[1], :z.shape[1]])
        a = jnp.where(mask, jnp.exp(z - cum), 0.0)
        acc_ref[...] = _bdot(a.astype(BF16), vn_ref[0].astype(BF16))
        carry_ref[...] = jnp.broadcast_to(cum[:, 0:1], carry_ref.shape)

    carry = carry_ref[:, 0:1]
    acc = acc_ref[...]
    for gi in reversed(range(p // 2)):
        kk = jnp.concatenate([k_refs[2 * gi][0], k_refs[2 * gi + 1][0]], axis=1).astype(BF16)
        vv = jnp.concatenate([v_refs[2 * gi][0], v_refs[2 * gi + 1][0]], axis=1).astype(BF16)
        z = _bdot(qbd, kk) + bias
        cum = _bdot(_softplus(z).astype(BF16), u)
        a = jnp.exp(z - cum - carry)
        acc = acc + lax.dot_general(a.astype(BF16), vv, nt, preferred_element_type=F32)
        carry = carry + cum[:, 0:1]
    acc_ref[...] = acc
    carry_ref[...] = jnp.broadcast_to(carry, carry_ref.shape)

    @pl.when(c == pl.num_programs(1) - 1)
    def _():
        row = lax.broadcasted_iota(jnp.int32, acc.shape, 0)
        col = lax.broadcasted_iota(jnp.int32, acc.shape, 1)
        own = jnp.where(row % n_heads == col // HEAD_DIM, acc, 0.0)
        for t in range(n_tok):
            o_ref[0, t:t + 1, :] = jnp.sum(own[t * n_heads:(t + 1) * n_heads], axis=0, keepdims=True)


def _attention_sample(page_table, qbd, bias_col, kn_pad, vn_pad, u, cache_k, cache_v, n_tok):
    db, n_pages = page_table.shape
    d = cache_k.shape[1]
    n_heads = d // HEAD_DIM
    p = PAGES_PER_STEP
    n_steps = n_pages // p

    def page_spec(i):
        return pl.BlockSpec((1, d, PAGE_SIZE),
                            lambda b, c, pt: (pt[b, (n_steps - 1 - c) * p + i], 0, 0))

    per_seq = lambda b, c, pt: (b, 0, 0)
    grid_spec = pltpu.PrefetchScalarGridSpec(
        num_scalar_prefetch=1,
        grid=(db, n_steps),
        in_specs=[pl.BlockSpec((1, DEC_ROWS, d), per_seq),
                  pl.BlockSpec(bias_col.shape, lambda b, c, pt: (0, 0)),
                  pl.BlockSpec((1, PAGE_SIZE, d), per_seq),
                  pl.BlockSpec((1, PAGE_SIZE, d), per_seq),
                  pl.BlockSpec(u.shape, lambda b, c, pt: (0, 0))]
                 + [page_spec(i) for i in range(p)] * 2,
        out_specs=pl.BlockSpec((1, n_tok, d), per_seq),
        scratch_shapes=[pltpu.VMEM((DEC_ROWS, V7X_LANES), F32), pltpu.VMEM((DEC_ROWS, d), F32)],
    )
    return pl.pallas_call(
        functools.partial(_decode_kernel, n_tok=n_tok, n_heads=n_heads),
        grid_spec=grid_spec,
        out_shape=jax.ShapeDtypeStruct((db, n_tok, d), F32),
        compiler_params=_cparams(("arbitrary", "arbitrary")),
        name="sb_attn_decode",
    )(page_table, qbd, bias_col, kn_pad, vn_pad, u, *([cache_k] * p), *([cache_v] * p))


def _suffix_sum_matrix(n, extra_ones_rows=0):
    r = lax.broadcasted_iota(jnp.int32, (n + extra_ones_rows, n), 0)
    c = lax.broadcasted_iota(jnp.int32, (n + extra_ones_rows, n), 1)
    return ((c >= r) | (r >= n)).astype(BF16)


def kernel(x_prompt, x_sample, state_pool, cache_k, cache_v, page_table, c_prompt, c_sample, w_ada, b_ada, norm_mix, norm_ffn, w_pool, pool_scale, w_qkv, q_norm, k_norm, sb_bias, w_o, w_gu, w_down):
    batch, seq, d = x_prompt.shape
    db, n_tok, _ = x_sample.shape
    depth = w_ada.shape[0]
    n_heads = d // HEAD_DIM
    n_pages = page_table.shape[1]
    past = n_pages * PAGE_SIZE
    assert batch == 1 and seq % 1024 == 0 and d % (2 * V7X_LANES) == 0
    assert n_pages % PAGES_PER_STEP == 0 and n_tok * n_heads <= DEC_ROWS and n_tok <= PAGE_SIZE
    assert (db * n_tok) % 16 == 0 and state_pool.shape[2] == POOL_HALO - 1

    c_all = jnp.concatenate([c_prompt, c_sample], axis=0)
    pad = (-c_all.shape[0]) % 8
    c_all = jnp.pad(c_all, ((0, pad), (0, 0)))
    mods = _ada(c_all, w_ada, b_ada)

    xp = x_prompt.reshape(seq, d)
    xs = x_sample.transpose(1, 0, 2).reshape(n_tok * db, d)

    seg = (lax.broadcasted_iota(jnp.int32, (d, V7X_LANES), 0) // HEAD_DIM
           == lax.broadcasted_iota(jnp.int32, (d, V7X_LANES), 1)).astype(BF16)
    seg_t3 = jnp.tile(seg.T, (3, 1))
    ut = _suffix_sum_matrix(ATT_TILE, 8)
    u = _suffix_sum_matrix(ATT_TILE).T

    w_gu_b, w_down_b = w_gu.astype(BF16), w_down.astype(BF16)
    w_pool_b, w_qkv_b, w_o_b = w_pool.astype(BF16), w_qkv.astype(BF16), w_o.astype(BF16)

    pool_p, pool_s, kp_l, vp_l, ks_l, vs_l = [], [], [], [], [], []
    for l in range(depth):
        m6 = mods[l].reshape(-1, 6, d)
        mp = [m6[0:1, i] for i in range(6)]
        ms = [m6[1:1 + db, i] for i in range(6)]
        ms_rows = [jnp.tile(a, (n_tok, 1)) for a in ms]
        g_mix, g_ffn = norm_mix[l][None, :], norm_ffn[l][None, :]
        j = l // 2
        if l % 2 == 0:
            xp, st_p = _pool_prompt(xp, mp[0], mp[1], mp[2], g_mix, w_pool_b[j], pool_scale[j][None, :])
            xs3, st_s = _pool_sample(xs.reshape(n_tok, db, d), state_pool[j].transpose(1, 0, 2),
                                     ms[0], ms[1], ms[2], g_mix, w_pool_b[j], pool_scale[j][None, :], past)
            xs = xs3.reshape(n_tok * db, d)
            pool_p.append(st_p[1:][None])
            pool_s.append(st_s.transpose(1, 0, 2))
        else:
            gq = jnp.tile(q_norm[j], n_heads)[None, :]
            gk = jnp.tile(k_norm[j], n_heads)[None, :]
            kp, vp, kb, q_t, v_t = _qkv(xp, mp[0], mp[1], g_mix, w_qkv_b[j], seg, seg_t3, gq, gk, True)
            o_t = _attention_prompt(sb_bias[j] * LOG2E, q_t, kb, v_t, ut)
            xp = _wo(o_t, xp, mp[2], w_o_b[j], True)

            qs, kn, vn = _qkv(xs, ms_rows[0], ms_rows[1], g_mix, w_qkv_b[j], seg, seg_t3, gq, gk, False)
            to_seq = lambda a: a.reshape(n_tok, db, d).transpose(1, 0, 2)
            q_seq = (to_seq(qs) * HEAD_DIM ** -0.5).reshape(db, n_tok, n_heads, HEAD_DIM)
            qbd = jnp.einsum('bthd,hg->bthgd', q_seq, jnp.eye(n_heads, dtype=F32))
            qbd = qbd.reshape(db, n_tok * n_heads, d)
            qbd = jnp.pad(qbd, ((0, 0), (0, DEC_ROWS - n_tok * n_heads), (0, 0))).astype(BF16)
            bias_col = jnp.tile(sb_bias[j], DEC_ROWS // n_heads)[:, None]
            pad_keys = lambda a: jnp.pad(to_seq(a), ((0, 0), (0, PAGE_SIZE - n_tok), (0, 0)))
            n_phys = cache_k.shape[1]
            pages_t = lambda c: c[j].transpose(0, 2, 3, 1).reshape(n_phys, d, PAGE_SIZE)
            o_s = _attention_sample(page_table, qbd, bias_col, pad_keys(kn), pad_keys(vn), u,
                                    pages_t(cache_k), pages_t(cache_v), n_tok)
            xs = _wo(o_s.transpose(1, 0, 2).reshape(n_tok * db, d), xs, ms_rows[2], w_o_b[j], False)

            kp_l.append(kp.reshape(batch, seq, n_heads, HEAD_DIM))
            vp_l.append(vp.reshape(batch, seq, n_heads, HEAD_DIM))
            ks_l.append(to_seq(kn).reshape(db, n_tok, n_heads, HEAD_DIM))
            vs_l.append(to_seq(vn).reshape(db, n_tok, n_heads, HEAD_DIM))
        xp = _ffn(xp, mp[3], mp[4], mp[5], g_ffn, w_gu_b[l], w_down_b[l])
        xs = _ffn(xs, ms_rows[3], ms_rows[4], ms_rows[5], g_ffn, w_gu_b[l], w_down_b[l])

    y_sample = xs.reshape(n_tok, db, d).transpose(1, 0, 2)
    return (xp.reshape(batch, seq, d), y_sample,
            jnp.stack(pool_p), jnp.stack(pool_s),
            jnp.stack(kp_l), jnp.stack(vp_l), jnp.stack(ks_l), jnp.stack(vs_l))
```

```python
import functools

import jax
import jax.numpy as jnp
from jax import lax
from jax.experimental import pallas as pl
from jax.experimental.pallas import tpu as pltpu

POOL_WINDOWS = (2, 4, 8, 16)
POOL_HALO = 16
HEAD_DIM = 64
PAGE_SIZE = 128
EPS = 1e-6
F32 = jnp.float32
BF16 = jnp.bfloat16

V7X_LANES = 128
ATT_TILE = 256
ATT_TQ = 512
LOG2E = 1.4426950408889634
BIAS_PIECES = 3
PAGES_PER_STEP = 8
DEC_ROWS = 128
VMEM_LIMIT = 52 * 1024 * 1024


def _cparams(sem):
    return pltpu.CompilerParams(dimension_semantics=sem, vmem_limit_bytes=VMEM_LIMIT)


def _modulate(x, g, shift, scale):
    ms = jnp.mean(x * x, axis=-1, keepdims=True)
    y = x * lax.rsqrt(ms + EPS) * g
    return y * (1.0 + scale) + shift


def _softplus(z):
    return jnp.maximum(z, 0.0) + jnp.log(1.0 + jnp.exp(-jnp.abs(z)))


def _bdot(a, b):
    return jnp.dot(a, b, preferred_element_type=F32)


def _ada_kernel(c_ref, w_ref, b_ref, o_ref):
    c = c_ref[...]
    s = c * jax.nn.sigmoid(c)
    o_ref[0] = jnp.dot(s, w_ref[0], preferred_element_type=F32,
                       precision=lax.Precision.HIGHEST) + b_ref[0]


def _ada(c_all, w_ada, b_ada):
    depth, d, n = w_ada.shape
    rows = c_all.shape[0]
    tn = n // 4
    return pl.pallas_call(
        _ada_kernel,
        grid=(depth, n // tn),
        in_specs=[pl.BlockSpec((rows, d), lambda l, j: (0, 0)),
                  pl.BlockSpec((1, d, tn), lambda l, j: (l, 0, j)),
                  pl.BlockSpec((1, 1, tn), lambda l, j: (l, 0, j))],
        out_specs=pl.BlockSpec((1, rows, tn), lambda l, j: (l, 0, j)),
        out_shape=jax.ShapeDtypeStruct((depth, rows, n), F32),
        compiler_params=_cparams(("arbitrary", "arbitrary")),
        name="ada_mod",
    )(c_all, w_ada, b_ada.reshape(depth, 1, n))


def _pool_groups(h, lookback, cnt_of, wp_ref):
    gd = h.shape[1] // len(POOL_WINDOWS)
    outs = []
    for gi, w in enumerate(POOL_WINDOWS):
        c0 = gi * gd
        cur = h[:, c0:c0 + gd]
        s = cur
        for k in range(1, w):
            s = s + lookback(k, c0, gd)
        pooled = s / cnt_of(w) - cur
        outs.append(_bdot(pooled.astype(BF16), wp_ref[gi]))
    return jnp.concatenate(outs, axis=-1)


def _pool_prompt_kernel(x_ref, xprev_ref, sh_ref, sc_ref, gate_ref, g_ref, wp_ref, ps_ref,
                        o_ref, st_ref, ext_ref, *, tm):
    i = pl.program_id(0)
    g, sh, sc = g_ref[...], sh_ref[...], sc_ref[...]
    x = x_ref[...]
    h = _modulate(x, g, sh, sc)
    hp = _modulate(xprev_ref[...], g, sh, sc)
    ext_ref[0:POOL_HALO, :] = jnp.where(i > 0, hp, 0.0)
    ext_ref[POOL_HALO:POOL_HALO + tm, :] = h
    st_ref[...] = h[tm - POOL_HALO:, :]
    pos = i * tm + lax.broadcasted_iota(jnp.int32, (tm, 1), 0)

    def lookback(k, c0, gd):
        return ext_ref[POOL_HALO - k:POOL_HALO - k + tm, c0:c0 + gd]

    def cnt_of(w):
        return jnp.minimum(pos + 1, w).astype(F32)

    out = _pool_groups(h, lookback, cnt_of, wp_ref) * ps_ref[...]
    o_ref[...] = x + gate_ref[...] * out


def _pool_prompt(x, sh, sc, gate, g, wp, ps):
    m, d = x.shape
    tm = min(512, m)
    blocks_per_tile = tm // POOL_HALO
    row = lambda i: (0, 0)
    return pl.pallas_call(
        functools.partial(_pool_prompt_kernel, tm=tm),
        grid=(m // tm,),
        in_specs=[pl.BlockSpec((tm, d), lambda i: (i, 0)),
                  pl.BlockSpec((POOL_HALO, d), lambda i: (jnp.maximum(i * blocks_per_tile - 1, 0), 0)),
                  pl.BlockSpec((1, d), row), pl.BlockSpec((1, d), row), pl.BlockSpec((1, d), row),
                  pl.BlockSpec((1, d), row),
                  pl.BlockSpec(wp.shape, lambda i: (0, 0, 0)),
                  pl.BlockSpec((1, d), row)],
        out_specs=[pl.BlockSpec((tm, d), lambda i: (i, 0)),
                   pl.BlockSpec((POOL_HALO, d), row)],
        out_shape=[jax.ShapeDtypeStruct((m, d), F32),
                   jax.ShapeDtypeStruct((POOL_HALO, d), F32)],
        scratch_shapes=[pltpu.VMEM((POOL_HALO + tm, d), F32)],
        compiler_params=_cparams(("arbitrary",)),
        name="pool_prompt",
    )(x, x, sh, sc, gate, g, wp, ps)


def _pool_sample_kernel(x_ref, st_ref, sh_ref, sc_ref, gate_ref, g_ref, wp_ref, ps_ref,
                        o_ref, nst_ref, *, past):
    n_tok, n_state = x_ref.shape[0], st_ref.shape[0]
    g, sh, sc = g_ref[...], sh_ref[...], sc_ref[...]
    hs = [_modulate(x_ref[t], g, sh, sc) for t in range(n_tok)]
    ext = [st_ref[r] for r in range(n_state)] + hs
    for t in range(n_tok):
        lookback = lambda k, c0, gd, t=t: ext[n_state + t - k][:, c0:c0 + gd]
        cnt_of = lambda w, t=t: float(min(past + t + 1, w))
        out = _pool_groups(hs[t], lookback, cnt_of, wp_ref) * ps_ref[...]
        o_ref[t] = x_ref[t] + gate_ref[...] * out
    for r in range(n_state):
        nst_ref[r] = ext[n_tok + r]


def _pool_sample(x_tm, state_tm, sh, sc, gate, g, wp, ps, past):
    return pl.pallas_call(
        functools.partial(_pool_sample_kernel, past=past),
        out_shape=[jax.ShapeDtypeStruct(x_tm.shape, F32),
                   jax.ShapeDtypeStruct(state_tm.shape, F32)],
        compiler_params=pltpu.CompilerParams(vmem_limit_bytes=VMEM_LIMIT),
        name="pool_sample",
    )(x_tm, state_tm, sh, sc, gate, g, wp, ps)


def _ffn_kernel(x_ref, sh_ref, sc_ref, gate_ref, g_ref, wg_ref, wu_ref, wd_ref, o_ref,
                hb_ref, acc_ref):
    j = pl.program_id(1)

    @pl.when(j == 0)
    def _():
        hb_ref[...] = _modulate(x_ref[...], g_ref[...], sh_ref[...], sc_ref[...]).astype(BF16)
        acc_ref[...] = jnp.zeros_like(acc_ref)

    hb = hb_ref[...]
    gg = _bdot(hb, wg_ref[...])
    uu = _bdot(hb, wu_ref[...])
    act = (gg * jax.nn.sigmoid(gg)) * uu
    acc_ref[...] += _bdot(act.astype(BF16), wd_ref[...])

    @pl.when(j == pl.num_programs(1) - 1)
    def _():
        o_ref[...] = x_ref[...] + gate_ref[...] * acc_ref[...]


def _mod_spec(mod, tm, d):
    if mod.shape[0] == 1:
        return pl.BlockSpec((1, d), lambda i, *_: (0, 0))
    return pl.BlockSpec((tm, d), lambda i, *_: (i, 0))


def _ffn(x, sh, sc, gate, g, w_gu, w_down):
    m, d = x.shape
    f = w_down.shape[0]
    tm = min(1024, m)
    tf = 256 if f % 256 == 0 else f
    nf = f // tf
    return pl.pallas_call(
        _ffn_kernel,
        grid=(m // tm, nf),
        in_specs=[pl.BlockSpec((tm, d), lambda i, j: (i, 0)),
                  _mod_spec(sh, tm, d), _mod_spec(sc, tm, d), _mod_spec(gate, tm, d),
                  pl.BlockSpec((1, d), lambda i, j: (0, 0)),
                  pl.BlockSpec((d, tf), lambda i, j: (0, j)),
                  pl.BlockSpec((d, tf), lambda i, j: (0, nf + j)),
                  pl.BlockSpec((tf, d), lambda i, j: (j, 0))],
        out_specs=pl.BlockSpec((tm, d), lambda i, j: (i, 0)),
        out_shape=jax.ShapeDtypeStruct((m, d), F32),
        scratch_shapes=[pltpu.VMEM((tm, d), BF16), pltpu.VMEM((tm, d), F32)],
        compiler_params=_cparams(("arbitrary", "arbitrary")),
        name="ffn",
    )(x, sh, sc, gate, g, w_gu, w_gu, w_down)


def _split3(r):
    r1 = r.astype(BF16)
    e1 = r - r1.astype(F32)
    r2 = e1.astype(BF16)
    r3 = (e1 - r2.astype(F32)).astype(BF16)
    return r1, r2, r3


def _head_rms(q, seg, seg_t3, gain):
    sq = q * q
    s1, s2, s3 = _split3(sq)
    ssum = _bdot(s1, seg) + _bdot(s2, seg) + _bdot(s3, seg)
    r = lax.rsqrt(ssum * (1.0 / HEAD_DIM) + EPS)
    rb = _bdot(jnp.concatenate(_split3(r), axis=-1), seg_t3)
    return q * rb * gain


def _qkv_kernel(x_ref, sh_ref, sc_ref, g_ref, w_ref, seg_ref, segt_ref, gq_ref, gk_ref, *out_refs,
                attn_layout, scale):
    d = x_ref.shape[1]
    hb = _modulate(x_ref[...], g_ref[...], sh_ref[...], sc_ref[...]).astype(BF16)
    qkv = _bdot(hb, w_ref[...])
    seg, seg_t3 = seg_ref[...], segt_ref[...]
    qn = _head_rms(qkv[:, :d], seg, seg_t3, gq_ref[...])
    kn = _head_rms(qkv[:, d:2 * d], seg, seg_t3, gk_ref[...])
    v = qkv[:, 2 * d:]
    if not attn_layout:
        q_ref, k_ref, v_ref = out_refs
        q_ref[...] = qn
        k_ref[...] = kn
        v_ref[...] = v
        return
    k_ref, v_ref, kb_ref, qt_ref, vt_ref = out_refs
    k_ref[...] = kn
    v_ref[...] = v
    lane = lax.broadcasted_iota(jnp.int32, (1, 2 * HEAD_DIM), 1)
    ones_cols = ((lane >= HEAD_DIM) & (lane < HEAD_DIM + BIAS_PIECES)).astype(F32)
    for hh in range(d // HEAD_DIM):
        pair = kn[:, (hh // 2) * 2 * HEAD_DIM:(hh // 2 + 1) * 2 * HEAD_DIM]
        if hh % 2:
            pair = pltpu.roll(pair, HEAD_DIM, 1)
        kb_ref[:, hh * 2 * HEAD_DIM:(hh + 1) * 2 * HEAD_DIM] = jnp.where(lane < HEAD_DIM, pair, ones_cols).astype(BF16)
    qt_ref[...] = (qn * scale).T.astype(BF16)
    vt = v.T.astype(BF16)
    n_heads, n_chunks = vt_ref.shape[0], vt_ref.shape[1]
    for hh in range(n_heads):
        for c in range(n_chunks):
            vt_ref[hh, c] = vt[hh * HEAD_DIM:(hh + 1) * HEAD_DIM, c * ATT_TILE:(c + 1) * ATT_TILE]


def _qkv(x, sh, sc, g, w_qkv, seg, seg_t3, gq, gk, attn_layout):
    m, d = x.shape
    n_heads = d // HEAD_DIM
    tm = min(256, m)
    const = lambda i: (0, 0)
    rows = lambda i: (i, 0)
    if attn_layout:
        out_specs = [pl.BlockSpec((tm, d), rows), pl.BlockSpec((tm, d), rows),
                     pl.BlockSpec((tm, 2 * d), rows),
                     pl.BlockSpec((d, tm), lambda i: (0, i)),
                     pl.BlockSpec((n_heads, tm // ATT_TILE, HEAD_DIM, ATT_TILE), lambda i: (0, i, 0, 0))]
        out_shape = [jax.ShapeDtypeStruct((m, d), F32), jax.ShapeDtypeStruct((m, d), F32),
                     jax.ShapeDtypeStruct((m, 2 * d), BF16),
                     jax.ShapeDtypeStruct((d, m), BF16),
                     jax.ShapeDtypeStruct((n_heads, m // ATT_TILE, HEAD_DIM, ATT_TILE), BF16)]
    else:
        out_specs = [pl.BlockSpec((tm, d), rows)] * 3
        out_shape = [jax.ShapeDtypeStruct((m, d), F32)] * 3
    return pl.pallas_call(
        functools.partial(_qkv_kernel, attn_layout=attn_layout, scale=HEAD_DIM ** -0.5 * LOG2E),
        grid=(m // tm,),
        in_specs=[pl.BlockSpec((tm, d), rows),
                  _mod_spec(sh, tm, d), _mod_spec(sc, tm, d),
                  pl.BlockSpec((1, d), const),
                  pl.BlockSpec(w_qkv.shape, const),
                  pl.BlockSpec(seg.shape, const), pl.BlockSpec(seg_t3.shape, const),
                  pl.BlockSpec((1, d), const), pl.BlockSpec((1, d), const)],
        out_specs=out_specs,
        out_shape=out_shape,
        compiler_params=_cparams(("arbitrary",)),
        name="qkv_attn" if attn_layout else "qkv_plain",
    )(x, sh, sc, g, w_qkv, seg, seg_t3, gq, gk)


def _softplus2(z):
    return jnp.maximum(z, 0.0) + jnp.log2(1.0 + jnp.exp2(-jnp.abs(z)))


def _attn_kernel(bias_ref, qt_ref, k_ref, vt_ref, ut_ref, o_ref, qaug_ref, acc_ref, carry_ref, sc_ref,
                 a_ref, z_ref):
    t, tq = ATT_TILE, ATT_TQ
    halves = tq // t
    chunks = [(c // halves, c % halves, slice(c * t, (c + 1) * t)) for c in range(2 * halves)]
    p, qi = pl.program_id(0), pl.program_id(1)
    row = lax.broadcasted_iota(jnp.int32, (HEAD_DIM, tq), 0)
    for head in range(2):
        b = jnp.full((HEAD_DIM, tq), bias_ref[2 * p + head], F32)
        pieces = _split3(b)
        aug = jnp.zeros((HEAD_DIM, tq), F32)
        for i in reversed(range(BIAS_PIECES)):
            aug = jnp.where(row == i, pieces[i].astype(F32), aug)
        qaug_ref[0:HEAD_DIM, head * tq:(head + 1) * tq] = qt_ref[head * HEAD_DIM:(head + 1) * HEAD_DIM, :]
        qaug_ref[HEAD_DIM:, head * tq:(head + 1) * tq] = aug.astype(BF16)
    acc_ref[...] = jnp.zeros_like(acc_ref)
    carry_ref[...] = jnp.zeros_like(carry_ref)
    ut = ut_ref[...]
    key_pos = lax.broadcasted_iota(jnp.int32, (t, t), 0)
    qry_pos = lax.broadcasted_iota(jnp.int32, (t, t), 1)


    def logits(kv):
        kt = k_ref[pl.ds(pl.multiple_of(kv * t, t), t), :]
        return [_bdot(kt[:, head * 2 * HEAD_DIM:(head + 1) * 2 * HEAD_DIM], qaug_ref[:, cols])
                for head, _, cols in chunks]

    def weights(kv, zs, masked, between=None):
        masks = [key_pos + (kv * t - qi * tq - half * t) < qry_pos if masked else None
                 for _, half, _ in chunks]
        sps = []
        for z, mask in zip(zs, masks):
            sp = _softplus2(z)
            if masked:
                sp = jnp.where(mask, sp, 0.0)
            sps.append(sp.astype(BF16))
        cum_tots = [_bdot(ut, sp) for sp in sps]
        if between is not None:
            between()
        for (_, _, cols), z, mask, cum_tot in zip(chunks, zs, masks, cum_tots):
            a = jnp.exp2(z - cum_tot[:t])
            if masked:
                a = jnp.where(mask, a, 0.0)
            a_ref[:, cols] = a.astype(BF16)
            carry = carry_ref[0:1, cols]
            sc_ref[0:1, cols] = jnp.exp2(-carry)
            carry_ref[0:1, cols] = carry + cum_tot[t:t + 1]

    def flush(kv):
        for head, half, cols in chunks:
            rows = slice(head * HEAD_DIM, (head + 1) * HEAD_DIM)
            acc_ref[rows, half * t:(half + 1) * t] += _bdot(vt_ref[head, kv], a_ref[:, cols]) * sc_ref[0:1, cols]

    def stash_logits(slot, kv):
        for (_, _, cols), z in zip(chunks, logits(kv)):
            z_ref[slot, :, cols] = z

    def step(slot, kv):
        zs = [z_ref[slot, :, cols] for _, _, cols in chunks]
        stash_logits(1 - slot, jnp.maximum(kv - 1, 0))
        flush(kv + 1)
        weights(kv, zs, False)

    assert halves == 2
    n_plain = qi * halves
    for i in reversed(range(halves)):
        kv = n_plain + i
        if i < halves - 1:
            flush(kv + 1)
        weights(kv, logits(kv), True)
    stash_logits(0, jnp.maximum(n_plain - 1, 0))

    def body(j, _):
        kv = n_plain - 1 - 2 * j
        step(0, kv)
        step(1, kv - 1)
        return 0

    lax.fori_loop(0, qi, body, 0)
    flush(0)
    o_ref[...] = acc_ref[...]


def _attention_prompt(bias2, q_t, kb, v_t, ut):
    d, s = q_t.shape
    n_pairs = d // (2 * HEAD_DIM)
    t, tq = ATT_TILE, ATT_TQ
    grid_spec = pltpu.PrefetchScalarGridSpec(
        num_scalar_prefetch=1,
        grid=(n_pairs, s // tq),
        in_specs=[pl.BlockSpec((2 * HEAD_DIM, tq), lambda p, qi, b: (p, qi)),
                  pl.BlockSpec((s, 4 * HEAD_DIM), lambda p, qi, b: (0, p)),
                  pl.BlockSpec((2, s // t, HEAD_DIM, t), lambda p, qi, b: (p, 0, 0, 0)),
                  pl.BlockSpec(ut.shape, lambda p, qi, b: (0, 0))],
        out_specs=pl.BlockSpec((2 * HEAD_DIM, tq), lambda p, qi, b: (p, qi)),
        scratch_shapes=[pltpu.VMEM((2 * HEAD_DIM, 2 * tq), BF16),
                        pltpu.VMEM((2 * HEAD_DIM, tq), F32),
                        pltpu.VMEM((8, 2 * tq), F32), pltpu.VMEM((8, 2 * tq), F32),
                        pltpu.VMEM((t, 2 * tq), BF16),
                        pltpu.VMEM((2, t, 2 * tq), F32)],
    )
    return pl.pallas_call(
        _attn_kernel,
        grid_spec=grid_spec,
        out_shape=jax.ShapeDtypeStruct((d, s), F32),
        compiler_params=_cparams(("arbitrary", "arbitrary")),
        name="sb_attn_prompt",
    )(bias2, q_t, kb, v_t, ut)


def _wo_kernel(o_ref, x_ref, gate_ref, w_ref, out_ref, *, transposed):
    o = o_ref[...]
    if transposed:
        o = o.T
    out_ref[...] = x_ref[...] + gate_ref[...] * _bdot(o.astype(BF16), w_ref[...])


def _wo(o, x, gate, w_o, transposed):
    m, d = x.shape
    tm = min(512, m)
    o_spec = pl.BlockSpec((d, tm), lambda i: (0, i)) if transposed else pl.BlockSpec((tm, d), lambda i: (i, 0))
    return pl.pallas_call(
        functools.partial(_wo_kernel, transposed=transposed),
        grid=(m // tm,),
        in_specs=[o_spec, pl.BlockSpec((tm, d), lambda i: (i, 0)), _mod_spec(gate, tm, d),
                  pl.BlockSpec((d, d), lambda i: (0, 0))],
        out_specs=pl.BlockSpec((tm, d), lambda i: (i, 0)),
        out_shape=jax.ShapeDtypeStruct((m, d), F32),
        compiler_params=_cparams(("arbitrary",)),
        name="wo_t" if transposed else "wo",
    )(o, x, gate, w_o)


def _decode_kernel(pt_ref, qbd_ref, bias_ref, kn_ref, vn_ref, u_ref, *refs, n_tok, n_heads):
    p = PAGES_PER_STEP
    k_refs, v_refs = refs[:p], refs[p:2 * p]
    o_ref, carry_ref, acc_ref = refs[2 * p:]
    c = pl.program_id(1)
    qbd = qbd_ref[0]
    bias = bias_ref[...]
    u = u_ref[...]
    nt = (((1,), (1,)), ((), ()))

    @pl.when(c == 0)
    def _():
        z = lax.dot_general(qbd, kn_ref[0].astype(BF16), nt, preferred_element_type=F32) + bias
        row = lax.broadcasted_iota(jnp.int32, z.shape, 0)
        col = lax.broadcasted_iota(jnp.int32, z.shape, 1)
        mask = (col < row // n_heads) & (col < n_tok)
        sp = jnp.where(mask, _softplus(z), 0.0)
        cum = _bdot(sp.astype(BF16), u[:z.shape[1], :z.shape[1]])
        a = jnp.where(mask, jnp.exp(z - cum), 0.0)
        acc_ref[...] = _bdot(a.astype(BF16), vn_ref[0].astype(BF16))
        carry_ref[...] = jnp.broadcast_to(cum[:, 0:1], carry_ref.shape)

    carry = carry_ref[:, 0:1]
    acc = acc_ref[...]
    for gi in reversed(range(p // 2)):
        kk = jnp.concatenate([k_refs[2 * gi][0], k_refs[2 * gi + 1][0]], axis=1).astype(BF16)
        vv = jnp.concatenate([v_refs[2 * gi][0], v_refs[2 * gi + 1][0]], axis=1).astype(BF16)
        z = _bdot(qbd, kk) + bias
        cum = _bdot(_softplus(z).astype(BF16), u)
        a = jnp.exp(z - cum - carry)
        acc = acc + lax.dot_general(a.astype(BF16), vv, nt, preferred_element_type=F32)
        carry = carry + cum[:, 0:1]
    acc_ref[...] = acc
    carry_ref[...] = jnp.broadcast_to(carry, carry_ref.shape)

    @pl.when(c == pl.num_programs(1) - 1)
    def _():
        row = lax.broadcasted_iota(jnp.int32, acc.shape, 0)
        col = lax.broadcasted_iota(jnp.int32, acc.shape, 1)
        own = jnp.where(row % n_heads == col // HEAD_DIM, acc, 0.0)
        for t in range(n_tok):
            o_ref[0, t:t + 1, :] = jnp.sum(own[t * n_heads:(t + 1) * n_heads], axis=0, keepdims=True)


def _attention_sample(page_table, qbd, bias_col, kn_pad, vn_pad, u, cache_k, cache_v, n_tok):
    db, n_pages = page_table.shape
    d = cache_k.shape[1]
    n_heads = d // HEAD_DIM
    p = PAGES_PER_STEP
    n_steps = n_pages // p

    def page_spec(i):
        return pl.BlockSpec((1, d, PAGE_SIZE),
                            lambda b, c, pt: (pt[b, (n_steps - 1 - c) * p + i], 0, 0))

    per_seq = lambda b, c, pt: (b, 0, 0)
    grid_spec = pltpu.PrefetchScalarGridSpec(
        num_scalar_prefetch=1,
        grid=(db, n_steps),
        in_specs=[pl.BlockSpec((1, DEC_ROWS, d), per_seq),
                  pl.BlockSpec(bias_col.shape, lambda b, c, pt: (0, 0)),
                  pl.BlockSpec((1, PAGE_SIZE, d), per_seq),
                  pl.BlockSpec((1, PAGE_SIZE, d), per_seq),
                  pl.BlockSpec(u.shape, lambda b, c, pt: (0, 0))]
                 + [page_spec(i) for i in range(p)] * 2,
        out_specs=pl.BlockSpec((1, n_tok, d), per_seq),
        scratch_shapes=[pltpu.VMEM((DEC_ROWS, V7X_LANES), F32), pltpu.VMEM((DEC_ROWS, d), F32)],
    )
    return pl.pallas_call(
        functools.partial(_decode_kernel, n_tok=n_tok, n_heads=n_heads),
        grid_spec=grid_spec,
        out_shape=jax.ShapeDtypeStruct((db, n_tok, d), F32),
        compiler_params=_cparams(("arbitrary", "arbitrary")),
        name="sb_attn_decode",
    )(page_table, qbd, bias_col, kn_pad, vn_pad, u, *([cache_k] * p), *([cache_v] * p))


def _suffix_sum_matrix(n, extra_ones_rows=0):
    r = lax.broadcasted_iota(jnp.int32, (n + extra_ones_rows, n), 0)
    c = lax.broadcasted_iota(jnp.int32, (n + extra_ones_rows, n), 1)
    return ((c >= r) | (r >= n)).astype(BF16)


def kernel(x_prompt, x_sample, state_pool, cache_k, cache_v, page_table, c_prompt, c_sample, w_ada, b_ada, norm_mix, norm_ffn, w_pool, pool_scale, w_qkv, q_norm, k_norm, sb_bias, w_o, w_gu, w_down):
    batch, seq, d = x_prompt.shape
    db, n_tok, _ = x_sample.shape
    depth = w_ada.shape[0]
    n_heads = d // HEAD_DIM
    n_pages = page_table.shape[1]
    past = n_pages * PAGE_SIZE
    assert batch == 1 and seq % 1024 == 0 and d % (2 * V7X_LANES) == 0
    assert n_pages % PAGES_PER_STEP == 0 and n_tok * n_heads <= DEC_ROWS and n_tok <= PAGE_SIZE
    assert (db * n_tok) % 16 == 0 and state_pool.shape[2] == POOL_HALO - 1

    c_all = jnp.concatenate([c_prompt, c_sample], axis=0)
    pad = (-c_all.shape[0]) % 8
    c_all = jnp.pad(c_all, ((0, pad), (0, 0)))
    mods = _ada(c_all, w_ada, b_ada)

    xp = x_prompt.reshape(seq, d)
    xs = x_sample.transpose(1, 0, 2).reshape(n_tok * db, d)

    seg = (lax.broadcasted_iota(jnp.int32, (d, V7X_LANES), 0) // HEAD_DIM
           == lax.broadcasted_iota(jnp.int32, (d, V7X_LANES), 1)).astype(BF16)
    seg_t3 = jnp.tile(seg.T, (3, 1))
    ut = _suffix_sum_matrix(ATT_TILE, 8)
    u = _suffix_sum_matrix(ATT_TILE).T

    w_gu_b, w_down_b = w_gu.astype(BF16), w_down.astype(BF16)
    w_pool_b, w_qkv_b, w_o_b = w_pool.astype(BF16), w_qkv.astype(BF16), w_o.astype(BF16)

    pool_p, pool_s, kp_l, vp_l, ks_l, vs_l = [], [], [], [], [], []
    for l in range(depth):
        m6 = mods[l].reshape(-1, 6, d)
        mp = [m6[0:1, i] for i in range(6)]
        ms = [m6[1:1 + db, i] for i in range(6)]
        ms_rows = [jnp.tile(a, (n_tok, 1)) for a in ms]
        g_mix, g_ffn = norm_mix[l][None, :], norm_ffn[l][None, :]
        j = l // 2
        if l % 2 == 0:
            xp, st_p = _pool_prompt(xp, mp[0], mp[1], mp[2], g_mix, w_pool_b[j], pool_scale[j][None, :])
            xs3, st_s = _pool_sample(xs.reshape(n_tok, db, d), state_pool[j].transpose(1, 0, 2),
                                     ms[0], ms[1], ms[2], g_mix, w_pool_b[j], pool_scale[j][None, :], past)
            xs = xs3.reshape(n_tok * db, d)
            pool_p.append(st_p[1:][None])
            pool_s.append(st_s.transpose(1, 0, 2))
        else:
            gq = jnp.tile(q_norm[j], n_heads)[None, :]
            gk = jnp.tile(k_norm[j], n_heads)[None, :]
            kp, vp, kb, q_t, v_t = _qkv(xp, mp[0], mp[1], g_mix, w_qkv_b[j], seg, seg_t3, gq, gk, True)
            o_t = _attention_prompt(sb_bias[j] * LOG2E, q_t, kb, v_t, ut)
            xp = _wo(o_t, xp, mp[2], w_o_b[j], True)

            qs, kn, vn = _qkv(xs, ms_rows[0], ms_rows[1], g_mix, w_qkv_b[j], seg, seg_t3, gq, gk, False)
            to_seq = lambda a: a.reshape(n_tok, db, d).transpose(1, 0, 2)
            q_seq = (to_seq(qs) * HEAD_DIM ** -0.5).reshape(db, n_tok, n_heads, HEAD_DIM)
            qbd = jnp.einsum('bthd,hg->bthgd', q_seq, jnp.eye(n_heads, dtype=F32))
            qbd = qbd.reshape(db, n_tok * n_heads, d)
            qbd = jnp.pad(qbd, ((0, 0), (0, DEC_ROWS - n_tok * n_heads), (0, 0))).astype(BF16)
            bias_col = jnp.tile(sb_bias[j], DEC_ROWS // n_heads)[:, None]
            pad_keys = lambda a: jnp.pad(to_seq(a), ((0, 0), (0, PAGE_SIZE - n_tok), (0, 0)))
            n_phys = cache_k.shape[1]
            pages_t = lambda c: c[j].transpose(0, 2, 3, 1).reshape(n_phys, d, PAGE_SIZE)
            o_s = _attention_sample(page_table, qbd, bias_col, pad_keys(kn), pad_keys(vn), u,
                                    pages_t(cache_k), pages_t(cache_v), n_tok)
            xs = _wo(o_s.transpose(1, 0, 2).reshape(n_tok * db, d), xs, ms_rows[2], w_o_b[j], False)

            kp_l.append(kp.reshape(batch, seq, n_heads, HEAD_DIM))
            vp_l.append(vp.reshape(batch, seq, n_heads, HEAD_DIM))
            ks_l.append(to_seq(kn).reshape(db, n_tok, n_heads, HEAD_DIM))
            vs_l.append(to_seq(vn).reshape(db, n_tok, n_heads, HEAD_DIM))
        xp = _ffn(xp, mp[3], mp[4], mp[5], g_ffn, w_gu_b[l], w_down_b[l])
        xs = _ffn(xs, ms_rows[3], ms_rows[4], ms_rows[5], g_ffn, w_gu_b[l], w_down_b[l])

    y_sample = xs.reshape(n_tok, db, d).transpose(1, 0, 2)
    return (xp.reshape(batch, seq, d), y_sample,
            jnp.stack(pool_p), jnp.stack(pool_s),
            jnp.stack(kp_l), jnp.stack(vp_l), jnp.stack(ks_l), jnp.stack(vs_l))
```

```python
import functools

import jax
import jax.numpy as jnp
from jax import lax
from jax.experimental import pallas as pl
from jax.experimental.pallas import tpu as pltpu

POOL_WINDOWS = (2, 4, 8, 16)
POOL_HALO = 16
HEAD_DIM = 64
PAGE_SIZE = 128
EPS = 1e-6
F32 = jnp.float32
BF16 = jnp.bfloat16

V7X_LANES = 128
ATT_TILE = 256
ATT_TQ = 512
LOG2E = 1.4426950408889634
BIAS_PIECES = 3
PAGES_PER_STEP = 8
DEC_ROWS = 128
VMEM_LIMIT = 52 * 1024 * 1024


def _cparams(sem):
    return pltpu.CompilerParams(dimension_semantics=sem, vmem_limit_bytes=VMEM_LIMIT)


def _modulate(x, g, shift, scale):
    ms = jnp.mean(x * x, axis=-1, keepdims=True)
    y = x * lax.rsqrt(ms + EPS) * g
    return y * (1.0 + scale) + shift


def _softplus(z):
    return jnp.maximum(z, 0.0) + jnp.log(1.0 + jnp.exp(-jnp.abs(z)))


def _bdot(a, b):
    return jnp.dot(a, b, preferred_element_type=F32)


def _ada_kernel(c_ref, w_ref, b_ref, o_ref):
    c = c_ref[...]
    s = c * jax.nn.sigmoid(c)
    o_ref[0] = jnp.dot(s, w_ref[0], preferred_element_type=F32,
                       precision=lax.Precision.HIGHEST) + b_ref[0]


def _ada(c_all, w_ada, b_ada):
    depth, d, n = w_ada.shape
    rows = c_all.shape[0]
    tn = n // 4
    return pl.pallas_call(
        _ada_kernel,
        grid=(depth, n // tn),
        in_specs=[pl.BlockSpec((rows, d), lambda l, j: (0, 0)),
                  pl.BlockSpec((1, d, tn), lambda l, j: (l, 0, j)),
                  pl.BlockSpec((1, 1, tn), lambda l, j: (l, 0, j))],
        out_specs=pl.BlockSpec((1, rows, tn), lambda l, j: (l, 0, j)),
        out_shape=jax.ShapeDtypeStruct((depth, rows, n), F32),
        compiler_params=_cparams(("arbitrary", "arbitrary")),
        name="ada_mod",
    )(c_all, w_ada, b_ada.reshape(depth, 1, n))


def _pool_groups(h, lookback, cnt_of, wp_ref):
    gd = h.shape[1] // len(POOL_WINDOWS)
    outs = []
    for gi, w in enumerate(POOL_WINDOWS):
        c0 = gi * gd
        cur = h[:, c0:c0 + gd]
        s = cur
        for k in range(1, w):
            s = s + lookback(k, c0, gd)
        pooled = s / cnt_of(w) - cur
        outs.append(_bdot(pooled.astype(BF16), wp_ref[gi]))
    return jnp.concatenate(outs, axis=-1)


def _pool_prompt_kernel(x_ref, xprev_ref, sh_ref, sc_ref, gate_ref, g_ref, wp_ref, ps_ref,
                        o_ref, st_ref, ext_ref, *, tm):
    i = pl.program_id(0)
    g, sh, sc = g_ref[...], sh_ref[...], sc_ref[...]
    x = x_ref[...]
    h = _modulate(x, g, sh, sc)
    hp = _modulate(xprev_ref[...], g, sh, sc)
    ext_ref[0:POOL_HALO, :] = jnp.where(i > 0, hp, 0.0)
    ext_ref[POOL_HALO:POOL_HALO + tm, :] = h
    st_ref[...] = h[tm - POOL_HALO:, :]
    pos = i * tm + lax.broadcasted_iota(jnp.int32, (tm, 1), 0)

    def lookback(k, c0, gd):
        return ext_ref[POOL_HALO - k:POOL_HALO - k + tm, c0:c0 + gd]

    def cnt_of(w):
        return jnp.minimum(pos + 1, w).astype(F32)

    out = _pool_groups(h, lookback, cnt_of, wp_ref) * ps_ref[...]
    o_ref[...] = x + gate_ref[...] * out


def _pool_prompt(x, sh, sc, gate, g, wp, ps):
    m, d = x.shape
    tm = min(512, m)
    blocks_per_tile = tm // POOL_HALO
    row = lambda i: (0, 0)
    return pl.pallas_call(
        functools.partial(_pool_prompt_kernel, tm=tm),
        grid=(m // tm,),
        in_specs=[pl.BlockSpec((tm, d), lambda i: (i, 0)),
                  pl.BlockSpec((POOL_HALO, d), lambda i: (jnp.maximum(i * blocks_per_tile - 1, 0), 0)),
                  pl.BlockSpec((1, d), row), pl.BlockSpec((1, d), row), pl.BlockSpec((1, d), row),
                  pl.BlockSpec((1, d), row),
                  pl.BlockSpec(wp.shape, lambda i: (0, 0, 0)),
                  pl.BlockSpec((1, d), row)],
        out_specs=[pl.BlockSpec((tm, d), lambda i: (i, 0)),
                   pl.BlockSpec((POOL_HALO, d), row)],
        out_shape=[jax.ShapeDtypeStruct((m, d), F32),
                   jax.ShapeDtypeStruct((POOL_HALO, d), F32)],
        scratch_shapes=[pltpu.VMEM((POOL_HALO + tm, d), F32)],
        compiler_params=_cparams(("arbitrary",)),
        name="pool_prompt",
    )(x, x, sh, sc, gate, g, wp, ps)


def _pool_sample_kernel(x_ref, st_ref, sh_ref, sc_ref, gate_ref, g_ref, wp_ref, ps_ref,
                        o_ref, nst_ref, *, past):
    n_tok, n_state = x_ref.shape[0], st_ref.shape[0]
    g, sh, sc = g_ref[...], sh_ref[...], sc_ref[...]
    hs = [_modulate(x_ref[t], g, sh, sc) for t in range(n_tok)]
    ext = [st_ref[r] for r in range(n_state)] + hs
    for t in range(n_tok):
        lookback = lambda k, c0, gd, t=t: ext[n_state + t - k][:, c0:c0 + gd]
        cnt_of = lambda w, t=t: float(min(past + t + 1, w))
        out = _pool_groups(hs[t], lookback, cnt_of, wp_ref) * ps_ref[...]
        o_ref[t] = x_ref[t] + gate_ref[...] * out
    for r in range(n_state):
        nst_ref[r] = ext[n_tok + r]


def _pool_sample(x_tm, state_tm, sh, sc, gate, g, wp, ps, past):
    return pl.pallas_call(
        functools.partial(_pool_sample_kernel, past=past),
        out_shape=[jax.ShapeDtypeStruct(x_tm.shape, F32),
                   jax.ShapeDtypeStruct(state_tm.shape, F32)],
        compiler_params=pltpu.CompilerParams(vmem_limit_bytes=VMEM_LIMIT),
        name="pool_sample",
    )(x_tm, state_tm, sh, sc, gate, g, wp, ps)


def _ffn_kernel(x_ref, sh_ref, sc_ref, gate_ref, g_ref, wg_ref, wu_ref, wd_ref, o_ref,
                hb_ref, acc_ref):
    j = pl.program_id(1)

    @pl.when(j == 0)
    def _():
        hb_ref[...] = _modulate(x_ref[...], g_ref[...], sh_ref[...], sc_ref[...]).astype(BF16)
        acc_ref[...] = jnp.zeros_like(acc_ref)

    hb = hb_ref[...]
    gg = _bdot(hb, wg_ref[...])
    uu = _bdot(hb, wu_ref[...])
    act = (gg * jax.nn.sigmoid(gg)) * uu
    acc_ref[...] += _bdot(act.astype(BF16), wd_ref[...])

    @pl.when(j == pl.num_programs(1) - 1)
    def _():
        o_ref[...] = x_ref[...] + gate_ref[...] * acc_ref[...]


def _mod_spec(mod, tm, d):
    if mod.shape[0] == 1:
        return pl.BlockSpec((1, d), lambda i, *_: (0, 0))
    return pl.BlockSpec((tm, d), lambda i, *_: (i, 0))


def _ffn(x, sh, sc, gate, g, w_gu, w_down):
    m, d = x.shape
    f = w_down.shape[0]
    tm = min(512, m)
    tf = f // 2 if f % (2 * V7X_LANES) == 0 else f
    nf = f // tf
    return pl.pallas_call(
        _ffn_kernel,
        grid=(m // tm, nf),
        in_specs=[pl.BlockSpec((tm, d), lambda i, j: (i, 0)),
                  _mod_spec(sh, tm, d), _mod_spec(sc, tm, d), _mod_spec(gate, tm, d),
                  pl.BlockSpec((1, d), lambda i, j: (0, 0)),
                  pl.BlockSpec((d, tf), lambda i, j: (0, j)),
                  pl.BlockSpec((d, tf), lambda i, j: (0, nf + j)),
                  pl.BlockSpec((tf, d), lambda i, j: (j, 0))],
        out_specs=pl.BlockSpec((tm, d), lambda i, j: (i, 0)),
        out_shape=jax.ShapeDtypeStruct((m, d), F32),
        scratch_shapes=[pltpu.VMEM((tm, d), BF16), pltpu.VMEM((tm, d), F32)],
        compiler_params=_cparams(("arbitrary", "arbitrary")),
        name="ffn",
    )(x, sh, sc, gate, g, w_gu, w_gu, w_down)


def _split3(r):
    r1 = r.astype(BF16)
    e1 = r - r1.astype(F32)
    r2 = e1.astype(BF16)
    r3 = (e1 - r2.astype(F32)).astype(BF16)
    return r1, r2, r3


def _head_rms(q, seg, seg_t3, gain):
    sq = q * q
    s1, s2, s3 = _split3(sq)
    ssum = _bdot(s1, seg) + _bdot(s2, seg) + _bdot(s3, seg)
    r = lax.rsqrt(ssum * (1.0 / HEAD_DIM) + EPS)
    rb = _bdot(jnp.concatenate(_split3(r), axis=-1), seg_t3)
    return q * rb * gain


def _qkv_kernel(x_ref, sh_ref, sc_ref, g_ref, w_ref, seg_ref, segt_ref, gq_ref, gk_ref, *out_refs,
                attn_layout, scale):
    d = x_ref.shape[1]
    hb = _modulate(x_ref[...], g_ref[...], sh_ref[...], sc_ref[...]).astype(BF16)
    qkv = _bdot(hb, w_ref[...])
    seg, seg_t3 = seg_ref[...], segt_ref[...]
    qn = _head_rms(qkv[:, :d], seg, seg_t3, gq_ref[...])
    kn = _head_rms(qkv[:, d:2 * d], seg, seg_t3, gk_ref[...])
    v = qkv[:, 2 * d:]
    if not attn_layout:
        q_ref, k_ref, v_ref = out_refs
        q_ref[...] = qn
        k_ref[...] = kn
        v_ref[...] = v
        return
    k_ref, v_ref, kb_ref, qt_ref, vt_ref = out_refs
    k_ref[...] = kn
    v_ref[...] = v
    lane = lax.broadcasted_iota(jnp.int32, (1, 2 * HEAD_DIM), 1)
    ones_cols = ((lane >= HEAD_DIM) & (lane < HEAD_DIM + BIAS_PIECES)).astype(F32)
    for hh in range(d // HEAD_DIM):
        pair = kn[:, (hh // 2) * 2 * HEAD_DIM:(hh // 2 + 1) * 2 * HEAD_DIM]
        if hh % 2:
            pair = pltpu.roll(pair, HEAD_DIM, 1)
        kb_ref[:, hh * 2 * HEAD_DIM:(hh + 1) * 2 * HEAD_DIM] = jnp.where(lane < HEAD_DIM, pair, ones_cols).astype(BF16)
    qt_ref[...] = (qn * scale).T.astype(BF16)
    vt = v.T.astype(BF16)
    n_heads, n_chunks = vt_ref.shape[0], vt_ref.shape[1]
    for hh in range(n_heads):
        for c in range(n_chunks):
            vt_ref[hh, c] = vt[hh * HEAD_DIM:(hh + 1) * HEAD_DIM, c * ATT_TILE:(c + 1) * ATT_TILE]


def _qkv(x, sh, sc, g, w_qkv, seg, seg_t3, gq, gk, attn_layout):
    m, d = x.shape
    n_heads = d // HEAD_DIM
    tm = min(256, m)
    const = lambda i: (0, 0)
    rows = lambda i: (i, 0)
    if attn_layout:
        out_specs = [pl.BlockSpec((tm, d), rows), pl.BlockSpec((tm, d), rows),
                     pl.BlockSpec((tm, 2 * d), rows),
                     pl.BlockSpec((d, tm), lambda i: (0, i)),
                     pl.BlockSpec((n_heads, tm // ATT_TILE, HEAD_DIM, ATT_TILE), lambda i: (0, i, 0, 0))]
        out_shape = [jax.ShapeDtypeStruct((m, d), F32), jax.ShapeDtypeStruct((m, d), F32),
                     jax.ShapeDtypeStruct((m, 2 * d), BF16),
                     jax.ShapeDtypeStruct((d, m), BF16),
                     jax.ShapeDtypeStruct((n_heads, m // ATT_TILE, HEAD_DIM, ATT_TILE), BF16)]
    else:
        out_specs = [pl.BlockSpec((tm, d), rows)] * 3
        out_shape = [jax.ShapeDtypeStruct((m, d), F32)] * 3
    return pl.pallas_call(
        functools.partial(_qkv_kernel, attn_layout=attn_layout, scale=HEAD_DIM ** -0.5 * LOG2E),
        grid=(m // tm,),
        in_specs=[pl.BlockSpec((tm, d), rows),
                  _mod_spec(sh, tm, d), _mod_spec(sc, tm, d),
                  pl.BlockSpec((1, d), const),
                  pl.BlockSpec(w_qkv.shape, const),
                  pl.BlockSpec(seg.shape, const), pl.BlockSpec(seg_t3.shape, const),
                  pl.BlockSpec((1, d), const), pl.BlockSpec((1, d), const)],
        out_specs=out_specs,
        out_shape=out_shape,
        compiler_params=_cparams(("arbitrary",)),
        name="qkv_attn" if attn_layout else "qkv_plain",
    )(x, sh, sc, g, w_qkv, seg, seg_t3, gq, gk)


def _attn_kernel(bias_ref, qt_ref, k_ref, vt_ref, ut_ref, o_ref, qaug_ref, acc_ref, carry_ref, sc_ref,
                 a_ref, z_ref):
    t, tq = ATT_TILE, ATT_TQ
    halves = tq // t
    chunks = [(c // halves, c % halves, slice(c * t, (c + 1) * t)) for c in range(2 * halves)]
    p, qi = pl.program_id(0), pl.program_id(1)
    row = lax.broadcasted_iota(jnp.int32, (HEAD_DIM, tq), 0)
    for head in range(2):
        b = jnp.full((HEAD_DIM, tq), bias_ref[2 * p + head], F32)
        pieces = _split3(b)
        aug = jnp.zeros((HEAD_DIM, tq), F32)
        for i in reversed(range(BIAS_PIECES)):
            aug = jnp.where(row == i, pieces[i].astype(F32), aug)
        qaug_ref[0:HEAD_DIM, head * tq:(head + 1) * tq] = qt_ref[head * HEAD_DIM:(head + 1) * HEAD_DIM, :]
        qaug_ref[HEAD_DIM:, head * tq:(head + 1) * tq] = aug.astype(BF16)
    acc_ref[...] = jnp.zeros_like(acc_ref)
    carry_ref[...] = jnp.zeros_like(carry_ref)
    ut = ut_ref[...]
    key_pos = lax.broadcasted_iota(jnp.int32, (t, t), 0)
    qry_pos = lax.broadcasted_iota(jnp.int32, (t, t), 1)


    def logits(kv, active=chunks):
        kt = k_ref[pl.ds(pl.multiple_of(kv * t, t), t), :]
        return [_bdot(kt[:, head * 2 * HEAD_DIM:(head + 1) * 2 * HEAD_DIM], qaug_ref[:, cols])
                for head, _, cols in active]

    def weights(kv, zs, masked, active=chunks):
        masks = [key_pos + (kv * t - qi * tq - half * t) < qry_pos if masked else None
                 for _, half, _ in active]
        sps = []
        for z, mask in zip(zs, masks):
            sp = jnp.maximum(z, 0.0) + jnp.log2(1.0 + jnp.exp2(-jnp.abs(z)))
            if masked:
                sp = jnp.where(mask, sp, 0.0)
            sps.append(sp.astype(BF16))
        cum_tots = [_bdot(ut, sp) for sp in sps]
        for (_, _, cols), z, mask, cum_tot in zip(active, zs, masks, cum_tots):
            a = jnp.exp2(z - cum_tot[:t])
            if masked:
                a = jnp.where(mask, a, 0.0)
            a_ref[:, cols] = a.astype(BF16)
            carry = carry_ref[0:1, cols]
            sc_ref[0:1, cols] = jnp.exp2(-carry)
            carry_ref[0:1, cols] = carry + cum_tot[t:t + 1]

    def flush(kv, active=chunks):
        for head, half, cols in active:
            rows = slice(head * HEAD_DIM, (head + 1) * HEAD_DIM)
            acc_ref[rows, half * t:(half + 1) * t] += _bdot(vt_ref[head, kv], a_ref[:, cols]) * sc_ref[0:1, cols]

    def stash_logits(slot, kv):
        for (_, _, cols), z in zip(chunks, logits(kv)):
            z_ref[slot, :, cols] = z

    def step(slot, kv):
        zs = [z_ref[slot, :, cols] for _, _, cols in chunks]
        stash_logits(1 - slot, jnp.maximum(kv - 1, 0))
        flush(kv + 1)
        weights(kv, zs, False)

    assert halves == 2
    n_plain = qi * halves
    stash_logits(0, jnp.maximum(n_plain - 1, 0))
    pending = None
    for i in reversed(range(halves)):
        kv = n_plain + i
        active = [c for c in chunks if c[1] >= i]
        if pending is not None:
            flush(*pending)
        weights(kv, logits(kv, active), True, active)
        pending = (kv, active)

    def body(j, _):
        kv = n_plain - 1 - 2 * j
        step(0, kv)
        step(1, kv - 1)
        return 0

    lax.fori_loop(0, qi, body, 0)
    flush(0)
    o_ref[...] = acc_ref[...]


def _attention_prompt(bias2, q_t, kb, v_t, ut):
    d, s = q_t.shape
    n_pairs = d // (2 * HEAD_DIM)
    t, tq = ATT_TILE, ATT_TQ
    grid_spec = pltpu.PrefetchScalarGridSpec(
        num_scalar_prefetch=1,
        grid=(n_pairs, s // tq),
        in_specs=[pl.BlockSpec((2 * HEAD_DIM, tq), lambda p, qi, b: (p, qi)),
                  pl.BlockSpec((s, 4 * HEAD_DIM), lambda p, qi, b: (0, p)),
                  pl.BlockSpec((2, s // t, HEAD_DIM, t), lambda p, qi, b: (p, 0, 0, 0)),
                  pl.BlockSpec(ut.shape, lambda p, qi, b: (0, 0))],
        out_specs=pl.BlockSpec((2 * HEAD_DIM, tq), lambda p, qi, b: (p, qi)),
        scratch_shapes=[pltpu.VMEM((2 * HEAD_DIM, 2 * tq), BF16),
                        pltpu.VMEM((2 * HEAD_DIM, tq), F32),
                        pltpu.VMEM((8, 2 * tq), F32), pltpu.VMEM((8, 2 * tq), F32),
                        pltpu.VMEM((t, 2 * tq), BF16),
                        pltpu.VMEM((2, t, 2 * tq), F32)],
    )
    return pl.pallas_call(
        _attn_kernel,
        grid_spec=grid_spec,
        out_shape=jax.ShapeDtypeStruct((d, s), F32),
        compiler_params=_cparams(("arbitrary", "arbitrary")),
        name="sb_attn_prompt",
    )(bias2, q_t, kb, v_t, ut)


def _wo_kernel(o_ref, x_ref, gate_ref, w_ref, out_ref, *, transposed):
    o = o_ref[...]
    if transposed:
        o = o.T
    out_ref[...] = x_ref[...] + gate_ref[...] * _bdot(o.astype(BF16), w_ref[...])


def _wo(o, x, gate, w_o, transposed):
    m, d = x.shape
    tm = min(512, m)
    o_spec = pl.BlockSpec((d, tm), lambda i: (0, i)) if transposed else pl.BlockSpec((tm, d), lambda i: (i, 0))
    return pl.pallas_call(
        functools.partial(_wo_kernel, transposed=transposed),
        grid=(m // tm,),
        in_specs=[o_spec, pl.BlockSpec((tm, d), lambda i: (i, 0)), _mod_spec(gate, tm, d),
                  pl.BlockSpec((d, d), lambda i: (0, 0))],
        out_specs=pl.BlockSpec((tm, d), lambda i: (i, 0)),
        out_shape=jax.ShapeDtypeStruct((m, d), F32),
        compiler_params=_cparams(("arbitrary",)),
        name="wo_t" if transposed else "wo",
    )(o, x, gate, w_o)


def _decode_kernel(pt_ref, qbd_ref, bias_ref, kn_ref, vn_ref, u_ref, *refs, n_tok, n_heads):
    p = PAGES_PER_STEP
    k_refs, v_refs = refs[:p], refs[p:2 * p]
    o_ref, carry_ref, acc_ref = refs[2 * p:]
    c = pl.program_id(1)
    qbd = qbd_ref[0]
    bias = bias_ref[...]
    u = u_ref[...]
    nt = (((1,), (1,)), ((), ()))

    @pl.when(c == 0)
    def _():
        z = lax.dot_general(qbd, kn_ref[0].astype(BF16), nt, preferred_element_type=F32) + bias
        row = lax.broadcasted_iota(jnp.int32, z.shape, 0)
        col = lax.broadcasted_iota(jnp.int32, z.shape, 1)
        mask = (col < row // n_heads) & (col < n_tok)
        sp = jnp.where(mask, _softplus(z), 0.0)
        cum = _bdot(sp.astype(BF16), u[:z.shape[1], :z.shape[1]])
        a = jnp.where(mask, jnp.exp(z - cum), 0.0)
        acc_ref[...] = _bdot(vn_ref[0].astype(BF16), a.T.astype(BF16))
        carry_ref[...] = jnp.broadcast_to(cum[:, 0:1], carry_ref.shape)

    groups = list(reversed(range(p // 2)))
    pages = lambda refs, gi: jnp.concatenate([refs[2 * gi][0], refs[2 * gi + 1][0]], axis=1).astype(BF16)
    zs = [_bdot(qbd, pages(k_refs, gi)) + bias for gi in groups]
    cums = [_bdot(_softplus(z).astype(BF16), u) for z in zs]
    carry = carry_ref[:, 0:1]
    acc = acc_ref[...]
    for gi, z, cum in zip(groups, zs, cums):
        a = jnp.exp(z - cum - carry)
        acc = acc + _bdot(pages(v_refs, gi), a.T.astype(BF16))
        carry = carry + cum[:, 0:1]
    acc_ref[...] = acc
    carry_ref[...] = jnp.broadcast_to(carry, carry_ref.shape)

    @pl.when(c == pl.num_programs(1) - 1)
    def _():
        acc_t = acc.T
        row = lax.broadcasted_iota(jnp.int32, acc_t.shape, 0)
        col = lax.broadcasted_iota(jnp.int32, acc_t.shape, 1)
        own = jnp.where(row % n_heads == col // HEAD_DIM, acc_t, 0.0)
        for t in range(n_tok):
            o_ref[0, t:t + 1, :] = jnp.sum(own[t * n_heads:(t + 1) * n_heads], axis=0, keepdims=True)


def _attention_sample(page_table, qbd, bias_col, kn_pad, vn_pad, u, cache_k, cache_v, n_tok):
    db, n_pages = page_table.shape
    d = cache_k.shape[1]
    n_heads = d // HEAD_DIM
    p = PAGES_PER_STEP
    n_steps = n_pages // p

    def page_spec(i):
        return pl.BlockSpec((1, d, PAGE_SIZE),
                            lambda b, c, pt: (pt[b, (n_steps - 1 - c) * p + i], 0, 0))

    per_seq = lambda b, c, pt: (b, 0, 0)
    grid_spec = pltpu.PrefetchScalarGridSpec(
        num_scalar_prefetch=1,
        grid=(db, n_steps),
        in_specs=[pl.BlockSpec((1, DEC_ROWS, d), per_seq),
                  pl.BlockSpec(bias_col.shape, lambda b, c, pt: (0, 0)),
                  pl.BlockSpec((1, PAGE_SIZE, d), per_seq),
                  pl.BlockSpec((1, d, PAGE_SIZE), per_seq),
                  pl.BlockSpec(u.shape, lambda b, c, pt: (0, 0))]
                 + [page_spec(i) for i in range(p)] * 2,
        out_specs=pl.BlockSpec((1, n_tok, d), per_seq),
        scratch_shapes=[pltpu.VMEM((DEC_ROWS, V7X_LANES), F32), pltpu.VMEM((d, DEC_ROWS), F32)],
    )
    return pl.pallas_call(
        functools.partial(_decode_kernel, n_tok=n_tok, n_heads=n_heads),
        grid_spec=grid_spec,
        out_shape=jax.ShapeDtypeStruct((db, n_tok, d), F32),
        compiler_params=_cparams(("arbitrary", "arbitrary")),
        name="sb_attn_decode",
    )(page_table, qbd, bias_col, kn_pad, vn_pad, u, *([cache_k] * p), *([cache_v] * p))


def _suffix_sum_matrix(n, extra_ones_rows=0):
    r = lax.broadcasted_iota(jnp.int32, (n + extra_ones_rows, n), 0)
    c = lax.broadcasted_iota(jnp.int32, (n + extra_ones_rows, n), 1)
    return ((c >= r) | (r >= n)).astype(BF16)


def kernel(x_prompt, x_sample, state_pool, cache_k, cache_v, page_table, c_prompt, c_sample, w_ada, b_ada, norm_mix, norm_ffn, w_pool, pool_scale, w_qkv, q_norm, k_norm, sb_bias, w_o, w_gu, w_down):
    batch, seq, d = x_prompt.shape
    db, n_tok, _ = x_sample.shape
    depth = w_ada.shape[0]
    n_heads = d // HEAD_DIM
    n_pages = page_table.shape[1]
    past = n_pages * PAGE_SIZE
    assert batch == 1 and seq % 1024 == 0 and d % (2 * V7X_LANES) == 0
    assert n_pages % PAGES_PER_STEP == 0 and n_tok * n_heads <= DEC_ROWS and n_tok <= PAGE_SIZE
    assert (db * n_tok) % 16 == 0 and state_pool.shape[2] == POOL_HALO - 1

    c_all = jnp.concatenate([c_prompt, c_sample], axis=0)
    pad = (-c_all.shape[0]) % 8
    c_all = jnp.pad(c_all, ((0, pad), (0, 0)))
    mods = _ada(c_all, w_ada, b_ada)

    xp = x_prompt.reshape(seq, d)
    xs = x_sample.transpose(1, 0, 2).reshape(n_tok * db, d)

    seg = (lax.broadcasted_iota(jnp.int32, (d, V7X_LANES), 0) // HEAD_DIM
           == lax.broadcasted_iota(jnp.int32, (d, V7X_LANES), 1)).astype(BF16)
    seg_t3 = jnp.tile(seg.T, (3, 1))
    ut = _suffix_sum_matrix(ATT_TILE, 8)
    u = _suffix_sum_matrix(ATT_TILE).T

    w_gu_b, w_down_b = w_gu.astype(BF16), w_down.astype(BF16)
    w_pool_b, w_qkv_b, w_o_b = w_pool.astype(BF16), w_qkv.astype(BF16), w_o.astype(BF16)

    pool_p, pool_s, kp_l, vp_l, ks_l, vs_l = [], [], [], [], [], []
    for l in range(depth):
        m6 = mods[l].reshape(-1, 6, d)
        mp = [m6[0:1, i] for i in range(6)]
        ms = [m6[1:1 + db, i] for i in range(6)]
        ms_rows = [jnp.tile(a, (n_tok, 1)) for a in ms]
        g_mix, g_ffn = norm_mix[l][None, :], norm_ffn[l][None, :]
        j = l // 2
        if l % 2 == 0:
            xp, st_p = _pool_prompt(xp, mp[0], mp[1], mp[2], g_mix, w_pool_b[j], pool_scale[j][None, :])
            xs3, st_s = _pool_sample(xs.reshape(n_tok, db, d), state_pool[j].transpose(1, 0, 2),
                                     ms[0], ms[1], ms[2], g_mix, w_pool_b[j], pool_scale[j][None, :], past)
            xs = xs3.reshape(n_tok * db, d)
            pool_p.append(st_p[1:][None])
            pool_s.append(st_s.transpose(1, 0, 2))
        else:
            gq = jnp.tile(q_norm[j], n_heads)[None, :]
            gk = jnp.tile(k_norm[j], n_heads)[None, :]
            kp, vp, kb, q_t, v_t = _qkv(xp, mp[0], mp[1], g_mix, w_qkv_b[j], seg, seg_t3, gq, gk, True)
            o_t = _attention_prompt(sb_bias[j] * LOG2E, q_t, kb, v_t, ut)
            xp = _wo(o_t, xp, mp[2], w_o_b[j], True)

            qs, kn, vn = _qkv(xs, ms_rows[0], ms_rows[1], g_mix, w_qkv_b[j], seg, seg_t3, gq, gk, False)
            to_seq = lambda a: a.reshape(n_tok, db, d).transpose(1, 0, 2)
            q_seq = (to_seq(qs) * HEAD_DIM ** -0.5).reshape(db, n_tok, n_heads, HEAD_DIM)
            qbd = jnp.einsum('bthd,hg->bthgd', q_seq, jnp.eye(n_heads, dtype=F32))
            qbd = qbd.reshape(db, n_tok * n_heads, d)
            qbd = jnp.pad(qbd, ((0, 0), (0, DEC_ROWS - n_tok * n_heads), (0, 0))).astype(BF16)
            bias_col = jnp.tile(sb_bias[j], DEC_ROWS // n_heads)[:, None]
            pad_keys = lambda a: jnp.pad(to_seq(a), ((0, 0), (0, PAGE_SIZE - n_tok), (0, 0)))
            n_phys = cache_k.shape[1]
            pages_t = lambda c: c[j].transpose(0, 2, 3, 1).reshape(n_phys, d, PAGE_SIZE)
            o_s = _attention_sample(page_table, qbd, bias_col, pad_keys(kn), pad_keys(vn).transpose(0, 2, 1), u,
                                    pages_t(cache_k), pages_t(cache_v), n_tok)
            xs = _wo(o_s.transpose(1, 0, 2).reshape(n_tok * db, d), xs, ms_rows[2], w_o_b[j], False)

            kp_l.append(kp.reshape(batch, seq, n_heads, HEAD_DIM))
            vp_l.append(vp.reshape(batch, seq, n_heads, HEAD_DIM))
            ks_l.append(to_seq(kn).reshape(db, n_tok, n_heads, HEAD_DIM))
            vs_l.append(to_seq(vn).reshape(db, n_tok, n_heads, HEAD_DIM))
        xp = _ffn(xp, mp[3], mp[4], mp[5], g_ffn, w_gu_b[l], w_down_b[l])
        xs = _ffn(xs, ms_rows[3], ms_rows[4], ms_rows[5], g_ffn, w_gu_b[l], w_down_b[l])

    y_sample = xs.reshape(n_tok, db, d).transpose(1, 0, 2)
    return (xp.reshape(batch, seq, d), y_sample,
            jnp.stack(pool_p), jnp.stack(pool_s),
            jnp.stack(kp_l), jnp.stack(vp_l), jnp.stack(ks_l), jnp.stack(vs_l))
```

```python
import functools

import jax
import jax.numpy as jnp
from jax import lax
from jax.experimental import pallas as pl
from jax.experimental.pallas import tpu as pltpu

POOL_WINDOWS = (2, 4, 8, 16)
POOL_HALO = 16
HEAD_DIM = 64
PAGE_SIZE = 128
EPS = 1e-6
F32 = jnp.float32
BF16 = jnp.bfloat16

V7X_LANES = 128
ATT_TILE = 256
ATT_TQ = 512
LOG2E = 1.4426950408889634
BIAS_PIECES = 3
PAGES_PER_STEP = 16
DEC_ROWS = 128
VMEM_LIMIT = 52 * 1024 * 1024


def _cparams(sem):
    return pltpu.CompilerParams(dimension_semantics=sem, vmem_limit_bytes=VMEM_LIMIT)


def _modulate(x, g, shift, scale):
    ms = jnp.mean(x * x, axis=-1, keepdims=True)
    y = x * lax.rsqrt(ms + EPS) * g
    return y * (1.0 + scale) + shift


def _softplus(z):
    return jnp.maximum(z, 0.0) + jnp.log(1.0 + jnp.exp(-jnp.abs(z)))


def _bdot(a, b):
    return jnp.dot(a, b, preferred_element_type=F32)


def _ada_kernel(c_ref, w_ref, b_ref, o_ref):
    c = c_ref[...]
    s = c * jax.nn.sigmoid(c)
    o_ref[0] = jnp.dot(s, w_ref[0], preferred_element_type=F32,
                       precision=lax.Precision.HIGHEST) + b_ref[0]


def _ada(c_all, w_ada, b_ada):
    depth, d, n = w_ada.shape
    rows = c_all.shape[0]
    tn = n // 4
    return pl.pallas_call(
        _ada_kernel,
        grid=(depth, n // tn),
        in_specs=[pl.BlockSpec((rows, d), lambda l, j: (0, 0)),
                  pl.BlockSpec((1, d, tn), lambda l, j: (l, 0, j)),
                  pl.BlockSpec((1, 1, tn), lambda l, j: (l, 0, j))],
        out_specs=pl.BlockSpec((1, rows, tn), lambda l, j: (l, 0, j)),
        out_shape=jax.ShapeDtypeStruct((depth, rows, n), F32),
        compiler_params=_cparams(("arbitrary", "arbitrary")),
        name="ada_mod",
    )(c_all, w_ada, b_ada.reshape(depth, 1, n))


def _pool_groups(h, lookback, cnt_of, wp_ref):
    gd = h.shape[1] // len(POOL_WINDOWS)
    outs = []
    for gi, w in enumerate(POOL_WINDOWS):
        c0 = gi * gd
        cur = h[:, c0:c0 + gd]
        s = cur
        for k in range(1, w):
            s = s + lookback(k, c0, gd)
        pooled = s / cnt_of(w) - cur
        outs.append(_bdot(pooled.astype(BF16), wp_ref[gi]))
    return jnp.concatenate(outs, axis=-1)


def _pool_prompt_kernel(x_ref, xprev_ref, sh_ref, sc_ref, gate_ref, g_ref, wp_ref, ps_ref,
                        o_ref, st_ref, ext_ref, *, tm):
    i = pl.program_id(0)
    g, sh, sc = g_ref[...], sh_ref[...], sc_ref[...]
    x = x_ref[...]
    h = _modulate(x, g, sh, sc)
    hp = _modulate(xprev_ref[...], g, sh, sc)
    ext_ref[0:POOL_HALO, :] = jnp.where(i > 0, hp, 0.0)
    ext_ref[POOL_HALO:POOL_HALO + tm, :] = h
    st_ref[...] = h[tm - POOL_HALO:, :]
    pos = i * tm + lax.broadcasted_iota(jnp.int32, (tm, 1), 0)

    def lookback(k, c0, gd):
        return ext_ref[POOL_HALO - k:POOL_HALO - k + tm, c0:c0 + gd]

    def cnt_of(w):
        return jnp.minimum(pos + 1, w).astype(F32)

    out = _pool_groups(h, lookback, cnt_of, wp_ref) * ps_ref[...]
    o_ref[...] = x + gate_ref[...] * out


def _pool_prompt(x, sh, sc, gate, g, wp, ps):
    m, d = x.shape
    tm = min(512, m)
    blocks_per_tile = tm // POOL_HALO
    row = lambda i: (0, 0)
    return pl.pallas_call(
        functools.partial(_pool_prompt_kernel, tm=tm),
        grid=(m // tm,),
        in_specs=[pl.BlockSpec((tm, d), lambda i: (i, 0)),
                  pl.BlockSpec((POOL_HALO, d), lambda i: (jnp.maximum(i * blocks_per_tile - 1, 0), 0)),
                  pl.BlockSpec((1, d), row), pl.BlockSpec((1, d), row), pl.BlockSpec((1, d), row),
                  pl.BlockSpec((1, d), row),
                  pl.BlockSpec(wp.shape, lambda i: (0, 0, 0)),
                  pl.BlockSpec((1, d), row)],
        out_specs=[pl.BlockSpec((tm, d), lambda i: (i, 0)),
                   pl.BlockSpec((POOL_HALO, d), row)],
        out_shape=[jax.ShapeDtypeStruct((m, d), F32),
                   jax.ShapeDtypeStruct((POOL_HALO, d), F32)],
        scratch_shapes=[pltpu.VMEM((POOL_HALO + tm, d), F32)],
        compiler_params=_cparams(("arbitrary",)),
        name="pool_prompt",
    )(x, x, sh, sc, gate, g, wp, ps)


def _pool_sample_kernel(x_ref, st_ref, sh_ref, sc_ref, gate_ref, g_ref, wp_ref, ps_ref,
                        o_ref, nst_ref, *, past):
    n_tok, n_state = x_ref.shape[0], st_ref.shape[0]
    g, sh, sc = g_ref[...], sh_ref[...], sc_ref[...]
    hs = [_modulate(x_ref[t], g, sh, sc) for t in range(n_tok)]
    ext = [st_ref[r] for r in range(n_state)] + hs
    for t in range(n_tok):
        lookback = lambda k, c0, gd, t=t: ext[n_state + t - k][:, c0:c0 + gd]
        cnt_of = lambda w, t=t: float(min(past + t + 1, w))
        out = _pool_groups(hs[t], lookback, cnt_of, wp_ref) * ps_ref[...]
        o_ref[t] = x_ref[t] + gate_ref[...] * out
    for r in range(n_state):
        nst_ref[r] = ext[n_tok + r]


def _pool_sample(x_tm, state_tm, sh, sc, gate, g, wp, ps, past):
    return pl.pallas_call(
        functools.partial(_pool_sample_kernel, past=past),
        out_shape=[jax.ShapeDtypeStruct(x_tm.shape, F32),
                   jax.ShapeDtypeStruct(state_tm.shape, F32)],
        compiler_params=pltpu.CompilerParams(vmem_limit_bytes=VMEM_LIMIT),
        name="pool_sample",
    )(x_tm, state_tm, sh, sc, gate, g, wp, ps)


def _ffn_kernel(x_ref, sh_ref, sc_ref, gate_ref, g_ref, wg_ref, wu_ref, wd_ref, o_ref,
                hb_ref, acc_ref):
    j = pl.program_id(1)

    @pl.when(j == 0)
    def _():
        hb_ref[...] = _modulate(x_ref[...], g_ref[...], sh_ref[...], sc_ref[...]).astype(BF16)
        acc_ref[...] = jnp.zeros_like(acc_ref)

    hb = hb_ref[...]
    gg = _bdot(hb, wg_ref[...])
    uu = _bdot(hb, wu_ref[...])
    act = (gg * jax.nn.sigmoid(gg)) * uu
    acc_ref[...] += _bdot(act.astype(BF16), wd_ref[...])

    @pl.when(j == pl.num_programs(1) - 1)
    def _():
        o_ref[...] = x_ref[...] + gate_ref[...] * acc_ref[...]


def _mod_spec(mod, tm, d):
    if mod.shape[0] == 1:
        return pl.BlockSpec((1, d), lambda i, *_: (0, 0))
    return pl.BlockSpec((tm, d), lambda i, *_: (i, 0))


def _ffn(x, sh, sc, gate, g, w_gu, w_down):
    m, d = x.shape
    f = w_down.shape[0]
    tm = min(512, m)
    tf = f // 2 if f % (2 * V7X_LANES) == 0 else f
    nf = f // tf
    return pl.pallas_call(
        _ffn_kernel,
        grid=(m // tm, nf),
        in_specs=[pl.BlockSpec((tm, d), lambda i, j: (i, 0)),
                  _mod_spec(sh, tm, d), _mod_spec(sc, tm, d), _mod_spec(gate, tm, d),
                  pl.BlockSpec((1, d), lambda i, j: (0, 0)),
                  pl.BlockSpec((d, tf), lambda i, j: (0, j)),
                  pl.BlockSpec((d, tf), lambda i, j: (0, nf + j)),
                  pl.BlockSpec((tf, d), lambda i, j: (j, 0))],
        out_specs=pl.BlockSpec((tm, d), lambda i, j: (i, 0)),
        out_shape=jax.ShapeDtypeStruct((m, d), F32),
        scratch_shapes=[pltpu.VMEM((tm, d), BF16), pltpu.VMEM((tm, d), F32)],
        compiler_params=_cparams(("arbitrary", "arbitrary")),
        name="ffn",
    )(x, sh, sc, gate, g, w_gu, w_gu, w_down)


def _split3(r):
    r1 = r.astype(BF16)
    e1 = r - r1.astype(F32)
    r2 = e1.astype(BF16)
    r3 = (e1 - r2.astype(F32)).astype(BF16)
    return r1, r2, r3


def _head_rms(q, seg, seg_t3, gain):
    sq = q * q
    s1, s2, s3 = _split3(sq)
    ssum = _bdot(s1, seg) + _bdot(s2, seg) + _bdot(s3, seg)
    r = lax.rsqrt(ssum * (1.0 / HEAD_DIM) + EPS)
    rb = _bdot(jnp.concatenate(_split3(r), axis=-1), seg_t3)
    return q * rb * gain


def _qkv_kernel(x_ref, sh_ref, sc_ref, g_ref, w_ref, seg_ref, segt_ref, gq_ref, gk_ref, *out_refs,
                attn_layout, scale):
    d = x_ref.shape[1]
    hb = _modulate(x_ref[...], g_ref[...], sh_ref[...], sc_ref[...]).astype(BF16)
    qkv = _bdot(hb, w_ref[...])
    seg, seg_t3 = seg_ref[...], segt_ref[...]
    qn = _head_rms(qkv[:, :d], seg, seg_t3, gq_ref[...])
    kn = _head_rms(qkv[:, d:2 * d], seg, seg_t3, gk_ref[...])
    v = qkv[:, 2 * d:]
    if not attn_layout:
        q_ref, k_ref, v_ref = out_refs
        q_ref[...] = qn
        k_ref[...] = kn
        v_ref[...] = v
        return
    k_ref, v_ref, kb_ref, qt_ref, vt_ref = out_refs
    k_ref[...] = kn
    v_ref[...] = v
    lane = lax.broadcasted_iota(jnp.int32, (1, 2 * HEAD_DIM), 1)
    ones_cols = ((lane >= HEAD_DIM) & (lane < HEAD_DIM + BIAS_PIECES)).astype(F32)
    for hh in range(d // HEAD_DIM):
        pair = kn[:, (hh // 2) * 2 * HEAD_DIM:(hh // 2 + 1) * 2 * HEAD_DIM]
        if hh % 2:
            pair = pltpu.roll(pair, HEAD_DIM, 1)
        kb_ref[:, hh * 2 * HEAD_DIM:(hh + 1) * 2 * HEAD_DIM] = jnp.where(lane < HEAD_DIM, pair, ones_cols).astype(BF16)
    qt_ref[...] = (qn * scale).T.astype(BF16)
    vt = v.T.astype(BF16)
    n_heads, n_chunks = vt_ref.shape[0], vt_ref.shape[1]
    for hh in range(n_heads):
        for c in range(n_chunks):
            vt_ref[hh, c] = vt[hh * HEAD_DIM:(hh + 1) * HEAD_DIM, c * ATT_TILE:(c + 1) * ATT_TILE]


def _qkv(x, sh, sc, g, w_qkv, seg, seg_t3, gq, gk, attn_layout):
    m, d = x.shape
    n_heads = d // HEAD_DIM
    tm = min(256, m)
    const = lambda i: (0, 0)
    rows = lambda i: (i, 0)
    if attn_layout:
        out_specs = [pl.BlockSpec((tm, d), rows), pl.BlockSpec((tm, d), rows),
                     pl.BlockSpec((tm, 2 * d), rows),
                     pl.BlockSpec((d, tm), lambda i: (0, i)),
                     pl.BlockSpec((n_heads, tm // ATT_TILE, HEAD_DIM, ATT_TILE), lambda i: (0, i, 0, 0))]
        out_shape = [jax.ShapeDtypeStruct((m, d), F32), jax.ShapeDtypeStruct((m, d), F32),
                     jax.ShapeDtypeStruct((m, 2 * d), BF16),
                     jax.ShapeDtypeStruct((d, m), BF16),
                     jax.ShapeDtypeStruct((n_heads, m // ATT_TILE, HEAD_DIM, ATT_TILE), BF16)]
    else:
        out_specs = [pl.BlockSpec((tm, d), rows)] * 3
        out_shape = [jax.ShapeDtypeStruct((m, d), F32)] * 3
    return pl.pallas_call(
        functools.partial(_qkv_kernel, attn_layout=attn_layout, scale=HEAD_DIM ** -0.5 * LOG2E),
        grid=(m // tm,),
        in_specs=[pl.BlockSpec((tm, d), rows),
                  _mod_spec(sh, tm, d), _mod_spec(sc, tm, d),
                  pl.BlockSpec((1, d), const),
                  pl.BlockSpec(w_qkv.shape, const),
                  pl.BlockSpec(seg.shape, const), pl.BlockSpec(seg_t3.shape, const),
                  pl.BlockSpec((1, d), const), pl.BlockSpec((1, d), const)],
        out_specs=out_specs,
        out_shape=out_shape,
        compiler_params=_cparams(("arbitrary",)),
        name="qkv_attn" if attn_layout else "qkv_plain",
    )(x, sh, sc, g, w_qkv, seg, seg_t3, gq, gk)


def _attn_kernel(bias_ref, qt_ref, k_ref, vt_ref, ut_ref, o_ref, qaug_ref, acc_ref, carry_ref, sc_ref,
                 a_ref, z_ref):
    t, tq = ATT_TILE, ATT_TQ
    halves = tq // t
    chunks = [(c // halves, c % halves, slice(c * t, (c + 1) * t)) for c in range(2 * halves)]
    p, qi = pl.program_id(0), pl.program_id(1)
    row = lax.broadcasted_iota(jnp.int32, (HEAD_DIM, tq), 0)
    for head in range(2):
        b = jnp.full((HEAD_DIM, tq), bias_ref[2 * p + head], F32)
        pieces = _split3(b)
        aug = jnp.zeros((HEAD_DIM, tq), F32)
        for i in reversed(range(BIAS_PIECES)):
            aug = jnp.where(row == i, pieces[i].astype(F32), aug)
        qaug_ref[0:HEAD_DIM, head * tq:(head + 1) * tq] = qt_ref[head * HEAD_DIM:(head + 1) * HEAD_DIM, :]
        qaug_ref[HEAD_DIM:, head * tq:(head + 1) * tq] = aug.astype(BF16)
    acc_ref[...] = jnp.zeros_like(acc_ref)
    carry_ref[...] = jnp.zeros_like(carry_ref)
    ut = ut_ref[...]
    key_pos = lax.broadcasted_iota(jnp.int32, (t, t), 0)
    qry_pos = lax.broadcasted_iota(jnp.int32, (t, t), 1)


    def logits(kv, active=chunks):
        kt = k_ref[pl.ds(pl.multiple_of(kv * t, t), t), :]
        return [_bdot(kt[:, head * 2 * HEAD_DIM:(head + 1) * 2 * HEAD_DIM], qaug_ref[:, cols])
                for head, _, cols in active]

    def weights(kv, zs, masked, active=chunks):
        masks = [key_pos + (kv * t - qi * tq - half * t) < qry_pos if masked else None
                 for _, half, _ in active]
        sps = []
        for z, mask in zip(zs, masks):
            sp = jnp.maximum(z, 0.0) + jnp.log2(1.0 + jnp.exp2(-jnp.abs(z)))
            if masked:
                sp = jnp.where(mask, sp, 0.0)
            sps.append(sp.astype(BF16))
        cum_tots = [_bdot(ut, sp) for sp in sps]
        for (_, _, cols), z, mask, cum_tot in zip(active, zs, masks, cum_tots):
            a = jnp.exp2(z - cum_tot[:t])
            if masked:
                a = jnp.where(mask, a, 0.0)
            a_ref[:, cols] = a.astype(BF16)
            carry = carry_ref[0:1, cols]
            sc_ref[0:1, cols] = jnp.exp2(-carry)
            carry_ref[0:1, cols] = carry + cum_tot[t:t + 1]

    def flush(kv, active=chunks):
        for head, half, cols in active:
            rows = slice(head * HEAD_DIM, (head + 1) * HEAD_DIM)
            acc_ref[rows, half * t:(half + 1) * t] += _bdot(vt_ref[head, kv], a_ref[:, cols]) * sc_ref[0:1, cols]

    def stash_logits(slot, kv):
        for (_, _, cols), z in zip(chunks, logits(kv)):
            z_ref[slot, :, cols] = z

    def step(slot, kv):
        zs = [z_ref[slot, :, cols] for _, _, cols in chunks]
        stash_logits(1 - slot, jnp.maximum(kv - 1, 0))
        flush(kv + 1)
        weights(kv, zs, False)

    assert halves == 2
    n_plain = qi * halves
    stash_logits(0, jnp.maximum(n_plain - 1, 0))
    pending = None
    for i in reversed(range(halves)):
        kv = n_plain + i
        active = [c for c in chunks if c[1] >= i]
        if pending is not None:
            flush(*pending)
        weights(kv, logits(kv, active), True, active)
        pending = (kv, active)

    @pl.when(qi % 2 == 1)
    def _():
        step(0, n_plain - 1)
        step(1, n_plain - 2)

    n_quads = qi // 2

    def body(j, _):
        kv = 4 * (n_quads - j) - 1
        for i in range(4):
            step(i % 2, kv - i)
        return 0

    lax.fori_loop(0, n_quads, body, 0)
    flush(0)
    o_ref[...] = acc_ref[...]


def _attention_prompt(bias2, q_t, kb, v_t, ut):
    d, s = q_t.shape
    n_pairs = d // (2 * HEAD_DIM)
    t, tq = ATT_TILE, ATT_TQ
    grid_spec = pltpu.PrefetchScalarGridSpec(
        num_scalar_prefetch=1,
        grid=(n_pairs, s // tq),
        in_specs=[pl.BlockSpec((2 * HEAD_DIM, tq), lambda p, qi, b: (p, qi)),
                  pl.BlockSpec((s, 4 * HEAD_DIM), lambda p, qi, b: (0, p)),
                  pl.BlockSpec((2, s // t, HEAD_DIM, t), lambda p, qi, b: (p, 0, 0, 0)),
                  pl.BlockSpec(ut.shape, lambda p, qi, b: (0, 0))],
        out_specs=pl.BlockSpec((2 * HEAD_DIM, tq), lambda p, qi, b: (p, qi)),
        scratch_shapes=[pltpu.VMEM((2 * HEAD_DIM, 2 * tq), BF16),
                        pltpu.VMEM((2 * HEAD_DIM, tq), F32),
                        pltpu.VMEM((8, 2 * tq), F32), pltpu.VMEM((8, 2 * tq), F32),
                        pltpu.VMEM((t, 2 * tq), BF16),
                        pltpu.VMEM((2, t, 2 * tq), F32)],
    )
    return pl.pallas_call(
        _attn_kernel,
        grid_spec=grid_spec,
        out_shape=jax.ShapeDtypeStruct((d, s), F32),
        compiler_params=_cparams(("arbitrary", "arbitrary")),
        name="sb_attn_prompt",
    )(bias2, q_t, kb, v_t, ut)


def _wo_kernel(o_ref, x_ref, gate_ref, w_ref, out_ref, *, transposed):
    o = o_ref[...]
    if transposed:
        o = o.T
    out_ref[...] = x_ref[...] + gate_ref[...] * _bdot(o.astype(BF16), w_ref[...])


def _wo(o, x, gate, w_o, transposed):
    m, d = x.shape
    tm = min(512, m)
    o_spec = pl.BlockSpec((d, tm), lambda i: (0, i)) if transposed else pl.BlockSpec((tm, d), lambda i: (i, 0))
    return pl.pallas_call(
        functools.partial(_wo_kernel, transposed=transposed),
        grid=(m // tm,),
        in_specs=[o_spec, pl.BlockSpec((tm, d), lambda i: (i, 0)), _mod_spec(gate, tm, d),
                  pl.BlockSpec((d, d), lambda i: (0, 0))],
        out_specs=pl.BlockSpec((tm, d), lambda i: (i, 0)),
        out_shape=jax.ShapeDtypeStruct((m, d), F32),
        compiler_params=_cparams(("arbitrary",)),
        name="wo_t" if transposed else "wo",
    )(o, x, gate, w_o)


def _decode_kernel(pt_ref, qbd_ref, bias_ref, kn_ref, vn_ref, u_ref, *refs, n_tok, n_heads):
    p = PAGES_PER_STEP
    k_refs, v_refs = refs[:p], refs[p:2 * p]
    o_ref, carry_ref, acc_ref = refs[2 * p:]
    c = pl.program_id(1)
    qbd = qbd_ref[0]
    bias = bias_ref[...]
    u = u_ref[...]
    nt = (((1,), (1,)), ((), ()))

    @pl.when(c == 0)
    def _():
        z = lax.dot_general(qbd, kn_ref[0].astype(BF16), nt, preferred_element_type=F32) + bias
        row = lax.broadcasted_iota(jnp.int32, z.shape, 0)
        col = lax.broadcasted_iota(jnp.int32, z.shape, 1)
        mask = (col < row // n_heads) & (col < n_tok)
        sp = jnp.where(mask, _softplus(z), 0.0)
        cum = _bdot(sp.astype(BF16), u[:z.shape[1], :z.shape[1]])
        a = jnp.where(mask, jnp.exp(z - cum), 0.0)
        acc_ref[...] = _bdot(vn_ref[0].astype(BF16), a.T.astype(BF16))
        carry_ref[...] = jnp.broadcast_to(cum[:, 0:1], carry_ref.shape)

    groups = list(reversed(range(p // 2)))
    pages = lambda refs, gi: jnp.concatenate([refs[2 * gi][0], refs[2 * gi + 1][0]], axis=1).astype(BF16)
    zs = [_bdot(qbd, pages(k_refs, gi)) + bias for gi in groups]
    cums = [_bdot(_softplus(z).astype(BF16), u) for z in zs]
    carry = carry_ref[:, 0:1]
    acc = acc_ref[...]
    for gi, z, cum in zip(groups, zs, cums):
        a = jnp.exp(z - cum - carry)
        acc = acc + _bdot(pages(v_refs, gi), a.T.astype(BF16))
        carry = carry + cum[:, 0:1]
    acc_ref[...] = acc
    carry_ref[...] = jnp.broadcast_to(carry, carry_ref.shape)

    @pl.when(c == pl.num_programs(1) - 1)
    def _():
        acc_t = acc.T
        row = lax.broadcasted_iota(jnp.int32, acc_t.shape, 0)
        col = lax.broadcasted_iota(jnp.int32, acc_t.shape, 1)
        own = jnp.where(row % n_heads == col // HEAD_DIM, acc_t, 0.0)
        for t in range(n_tok):
            o_ref[0, t:t + 1, :] = jnp.sum(own[t * n_heads:(t + 1) * n_heads], axis=0, keepdims=True)


def _attention_sample(page_table, qbd, bias_col, kn_pad, vn_pad, u, cache_k, cache_v, n_tok):
    db, n_pages = page_table.shape
    d = cache_k.shape[1]
    n_heads = d // HEAD_DIM
    p = PAGES_PER_STEP
    n_steps = n_pages // p

    def page_spec(i):
        return pl.BlockSpec((1, d, PAGE_SIZE),
                            lambda b, c, pt: (pt[b, (n_steps - 1 - c) * p + i], 0, 0))

    per_seq = lambda b, c, pt: (b, 0, 0)
    grid_spec = pltpu.PrefetchScalarGridSpec(
        num_scalar_prefetch=1,
        grid=(db, n_steps),
        in_specs=[pl.BlockSpec((1, DEC_ROWS, d), per_seq),
                  pl.BlockSpec(bias_col.shape, lambda b, c, pt: (0, 0)),
                  pl.BlockSpec((1, PAGE_SIZE, d), per_seq),
                  pl.BlockSpec((1, d, PAGE_SIZE), per_seq),
                  pl.BlockSpec(u.shape, lambda b, c, pt: (0, 0))]
                 + [page_spec(i) for i in range(p)] * 2,
        out_specs=pl.BlockSpec((1, n_tok, d), per_seq),
        scratch_shapes=[pltpu.VMEM((DEC_ROWS, V7X_LANES), F32), pltpu.VMEM((d, DEC_ROWS), F32)],
    )
    return pl.pallas_call(
        functools.partial(_decode_kernel, n_tok=n_tok, n_heads=n_heads),
        grid_spec=grid_spec,
        out_shape=jax.ShapeDtypeStruct((db, n_tok, d), F32),
        compiler_params=_cparams(("arbitrary", "arbitrary")),
        name="sb_attn_decode",
    )(page_table, qbd, bias_col, kn_pad, vn_pad, u, *([cache_k] * p), *([cache_v] * p))


def _suffix_sum_matrix(n, extra_ones_rows=0):
    r = lax.broadcasted_iota(jnp.int32, (n + extra_ones_rows, n), 0)
    c = lax.broadcasted_iota(jnp.int32, (n + extra_ones_rows, n), 1)
    return ((c >= r) | (r >= n)).astype(BF16)


def kernel(x_prompt, x_sample, state_pool, cache_k, cache_v, page_table, c_prompt, c_sample, w_ada, b_ada, norm_mix, norm_ffn, w_pool, pool_scale, w_qkv, q_norm, k_norm, sb_bias, w_o, w_gu, w_down):
    batch, seq, d = x_prompt.shape
    db, n_tok, _ = x_sample.shape
    depth = w_ada.shape[0]
    n_heads = d // HEAD_DIM
    n_pages = page_table.shape[1]
    past = n_pages * PAGE_SIZE
    assert batch == 1 and seq % 1024 == 0 and d % (2 * V7X_LANES) == 0
    assert n_pages % PAGES_PER_STEP == 0 and n_tok * n_heads <= DEC_ROWS and n_tok <= PAGE_SIZE
    assert (db * n_tok) % 16 == 0 and state_pool.shape[2] == POOL_HALO - 1

    c_all = jnp.concatenate([c_prompt, c_sample], axis=0)
    pad = (-c_all.shape[0]) % 8
    c_all = jnp.pad(c_all, ((0, pad), (0, 0)))
    mods = _ada(c_all, w_ada, b_ada)

    xp = x_prompt.reshape(seq, d)
    xs = x_sample.transpose(1, 0, 2).reshape(n_tok * db, d)

    seg = (lax.broadcasted_iota(jnp.int32, (d, V7X_LANES), 0) // HEAD_DIM
           == lax.broadcasted_iota(jnp.int32, (d, V7X_LANES), 1)).astype(BF16)
    seg_t3 = jnp.tile(seg.T, (3, 1))
    ut = _suffix_sum_matrix(ATT_TILE, 8)
    u = _suffix_sum_matrix(ATT_TILE).T

    w_gu_b, w_down_b = w_gu.astype(BF16), w_down.astype(BF16)
    w_pool_b, w_qkv_b, w_o_b = w_pool.astype(BF16), w_qkv.astype(BF16), w_o.astype(BF16)

    pool_p, pool_s, kp_l, vp_l, ks_l, vs_l = [], [], [], [], [], []
    for l in range(depth):
        m6 = mods[l].reshape(-1, 6, d)
        mp = [m6[0:1, i] for i in range(6)]
        ms = [m6[1:1 + db, i] for i in range(6)]
        ms_rows = [jnp.tile(a, (n_tok, 1)) for a in ms]
        g_mix, g_ffn = norm_mix[l][None, :], norm_ffn[l][None, :]
        j = l // 2
        if l % 2 == 0:
            xp, st_p = _pool_prompt(xp, mp[0], mp[1], mp[2], g_mix, w_pool_b[j], pool_scale[j][None, :])
            xs3, st_s = _pool_sample(xs.reshape(n_tok, db, d), state_pool[j].transpose(1, 0, 2),
                                     ms[0], ms[1], ms[2], g_mix, w_pool_b[j], pool_scale[j][None, :], past)
            xs = xs3.reshape(n_tok * db, d)
            pool_p.append(st_p[1:][None])
            pool_s.append(st_s.transpose(1, 0, 2))
        else:
            gq = jnp.tile(q_norm[j], n_heads)[None, :]
            gk = jnp.tile(k_norm[j], n_heads)[None, :]
            kp, vp, kb, q_t, v_t = _qkv(xp, mp[0], mp[1], g_mix, w_qkv_b[j], seg, seg_t3, gq, gk, True)
            o_t = _attention_prompt(sb_bias[j] * LOG2E, q_t, kb, v_t, ut)
            xp = _wo(o_t, xp, mp[2], w_o_b[j], True)

            qs, kn, vn = _qkv(xs, ms_rows[0], ms_rows[1], g_mix, w_qkv_b[j], seg, seg_t3, gq, gk, False)
            to_seq = lambda a: a.reshape(n_tok, db, d).transpose(1, 0, 2)
            q_seq = (to_seq(qs) * HEAD_DIM ** -0.5).reshape(db, n_tok, n_heads, HEAD_DIM)
            qbd = jnp.einsum('bthd,hg->bthgd', q_seq, jnp.eye(n_heads, dtype=F32))
            qbd = qbd.reshape(db, n_tok * n_heads, d)
            qbd = jnp.pad(qbd, ((0, 0), (0, DEC_ROWS - n_tok * n_heads), (0, 0))).astype(BF16)
            bias_col = jnp.tile(sb_bias[j], DEC_ROWS // n_heads)[:, None]
            pad_keys = lambda a: jnp.pad(to_seq(a), ((0, 0), (0, PAGE_SIZE - n_tok), (0, 0)))
            n_phys = cache_k.shape[1]
            pages_t = lambda c: c[j].transpose(0, 2, 3, 1).reshape(n_phys, d, PAGE_SIZE)
            o_s = _attention_sample(page_table, qbd, bias_col, pad_keys(kn), pad_keys(vn).transpose(0, 2, 1), u,
                                    pages_t(cache_k), pages_t(cache_v), n_tok)
            xs = _wo(o_s.transpose(1, 0, 2).reshape(n_tok * db, d), xs, ms_rows[2], w_o_b[j], False)

            kp_l.append(kp.reshape(batch, seq, n_heads, HEAD_DIM))
            vp_l.append(vp.reshape(batch, seq, n_heads, HEAD_DIM))
            ks_l.append(to_seq(kn).reshape(db, n_tok, n_heads, HEAD_DIM))
            vs_l.append(to_seq(vn).reshape(db, n_tok, n_heads, HEAD_DIM))
        xp = _ffn(xp, mp[3], mp[4], mp[5], g_ffn, w_gu_b[l], w_down_b[l])
        xs = _ffn(xs, ms_rows[3], ms_rows[4], ms_rows[5], g_ffn, w_gu_b[l], w_down_b[l])

    y_sample = xs.reshape(n_tok, db, d).transpose(1, 0, 2)
    return (xp.reshape(batch, seq, d), y_sample,
            jnp.stack(pool_p), jnp.stack(pool_s),
            jnp.stack(kp_l), jnp.stack(vp_l), jnp.stack(ks_l), jnp.stack(vs_l))
```
